```python
import math
import jax, jax.numpy as jnp
from jax import lax
import numpy as np

D_MODEL = 1024
BATCH = 16
SEQ = 2048
DEPTH = 1

PLE_DIM = 256
HEAD_DIM = 64
ATTN_WIDTH = D_MODEL // 2
ATTN_HEADS = ATTN_WIDTH // HEAD_DIM
SSM_WIDTH = D_MODEL - ATTN_WIDTH
SSM_GROUP = 16
SSM_GROUPS = SSM_WIDTH // SSM_GROUP
SSM_STATE = 64
MIX_WIDTH = ATTN_WIDTH + SSM_WIDTH
IN_WIDTH = 4 * ATTN_WIDTH + 2 * SSM_WIDTH
DILATED_CONFIGS = ((128, 1), (512, 4), (2048, 16))
BLOCK_Q = 128
EPS = 1e-6
DT_MIN = 1e-3
DT_MAX = 1e-1

kernel_name = "hymba_s5_longnet_hybrid"


def rms_norm(x, gain):
    xf = x.astype(jnp.float32)
    y = xf * lax.rsqrt(jnp.mean(xf * xf, axis=-1, keepdims=True) + EPS) * gain.astype(jnp.float32)
    return y.astype(x.dtype)


def banded_causal_attention(q, k, v, w):
    n, l, dh = q.shape
    nb = -(-l // BLOCK_Q)
    lp = nb * BLOCK_Q
    nk = BLOCK_Q + w
    qb = jnp.pad(q, ((0, 0), (0, lp - l), (0, 0))).reshape(n, nb, BLOCK_Q, dh)
    kp = jnp.pad(k, ((0, 0), (w, lp - l), (0, 0)))
    vp = jnp.pad(v, ((0, 0), (w, lp - l), (0, 0)))
    block_start = jnp.arange(nb) * BLOCK_Q
    idx = block_start[:, None] + jnp.arange(nk)[None, :]
    kb = kp[:, idx]
    vb = vp[:, idx]
    s = jnp.einsum('nbqd,nbkd->nbqk', qb, kb)
    qi = jnp.arange(BLOCK_Q)[:, None]
    kj = jnp.arange(nk)[None, :]
    key_pos = block_start[:, None, None] + kj[None] - w
    valid = (kj >= qi)[None] & (kj <= qi + w)[None] & (key_pos >= 0)
    s = jnp.where(valid[None], s, -jnp.inf)
    m = jnp.max(s, axis=-1, keepdims=True)
    e = jnp.exp(s - m)
    den = jnp.sum(e, axis=-1)
    o = jnp.einsum('nbqk,nbkd->nbqd', e, vb) / den[..., None]
    lse = m[..., 0] + jnp.log(den)
    return o.reshape(n, lp, dh)[:, :l], lse.reshape(n, lp)[:, :l]


def dilated_window_attention(q, k, v, window, dilation):
    b, s, h, dh = q.shape
    l = s // dilation
    w = window // dilation

    def to_classes(t):
        return t.reshape(b, l, dilation, h, dh).transpose(0, 2, 3, 1, 4).reshape(b * dilation * h, l, dh)

    o, lse = banded_causal_attention(to_classes(q), to_classes(k), to_classes(v), w)
    o = o.reshape(b, dilation, h, l, dh).transpose(0, 3, 1, 2, 4).reshape(b, s, h, dh)
    lse = lse.reshape(b, dilation, h, l).transpose(0, 3, 1, 2).reshape(b, s, h)
    return o, lse


def dilated_mixture_attention(q, k, v, q_gain, k_gain):
    b, s, _ = q.shape

    def heads(t):
        return t.reshape(b, s, ATTN_HEADS, HEAD_DIM).astype(jnp.float32)

    qh = rms_norm(heads(q), q_gain) * (HEAD_DIM ** -0.5)
    kh = rms_norm(heads(k), k_gain)
    vh = heads(v)
    outs, lses = [], []
    for window, dilation in DILATED_CONFIGS:
        o, lse = dilated_window_attention(qh, kh, vh, window, dilation)
        outs.append(o)
        lses.append(lse)
    wts = jax.nn.softmax(jnp.stack(lses), axis=0)
    o = jnp.sum(wts[..., None] * jnp.stack(outs), axis=0)
    return o.reshape(b, s, ATTN_WIDTH).astype(q.dtype)


def s5_glu(u, lam_re, lam_im, log_dt, b_re, b_im, c_re, c_im, d_skip, w_glu, b_glu):
    f32 = jnp.float32
    bsz, s, _ = u.shape
    uf = u.astype(f32)
    ug = uf.reshape(bsz, s, SSM_GROUPS, SSM_GROUP)
    lam_re = lam_re.astype(f32); lam_im = lam_im.astype(f32)
    dt = jnp.exp(log_dt.astype(f32))[:, None]
    mag = jnp.exp(lam_re * dt)
    a_re = mag * jnp.cos(lam_im * dt)
    a_im = mag * jnp.sin(lam_im * dt)
    den = lam_re * lam_re + lam_im * lam_im
    num_re = a_re - 1.0
    coef_re = (num_re * lam_re + a_im * lam_im) / den
    coef_im = (a_im * lam_re - num_re * lam_im) / den
    b_re = b_re.astype(f32); b_im = b_im.astype(f32)
    bb_re = coef_re[..., None] * b_re - coef_im[..., None] * b_im
    bb_im = coef_re[..., None] * b_im + coef_im[..., None] * b_re
    bu_re = jnp.einsum('bsgc,gnc->bsgn', ug, bb_re)
    bu_im = jnp.einsum('bsgc,gnc->bsgn', ug, bb_im)
    ar = jnp.broadcast_to(a_re, bu_re.shape)
    ai = jnp.broadcast_to(a_im, bu_re.shape)

    def combine(left, right):
        ar1, ai1, br1, bi1 = left
        ar2, ai2, br2, bi2 = right
        return (ar2 * ar1 - ai2 * ai1,
                ar2 * ai1 + ai2 * ar1,
                ar2 * br1 - ai2 * bi1 + br2,
                ar2 * bi1 + ai2 * br1 + bi2)

    _, _, xr, xi = lax.associative_scan(combine, (ar, ai, bu_re, bu_im), axis=1)
    y = (jnp.einsum('bsgn,gcn->bsgc', xr, c_re.astype(f32))
         - jnp.einsum('bsgn,gcn->bsgc', xi, c_im.astype(f32))).reshape(bsz, s, SSM_WIDTH)
    y = y + d_skip.astype(f32) * uf
    yg = jax.nn.gelu(y, approximate=False)
    out = yg * jax.nn.sigmoid(yg @ w_glu.astype(f32) + b_glu.astype(f32))
    return out.astype(u.dtype)


def setup_inputs(seed: int = 0) -> dict:
    key = jax.random.key(seed)
    ks = jax.random.split(key, 24)
    f32 = jnp.float32
    nrm = lambda k, shape, scale: jax.random.normal(k, shape, f32) * scale
    x = nrm(ks[0], (BATCH, SEQ, D_MODEL), 1.0)
    p = nrm(ks[1], (DEPTH, BATCH, SEQ, PLE_DIM), 1.0)
    mix_norm = 1.0 + nrm(ks[2], (DEPTH, D_MODEL), 0.02)
    w_in = nrm(ks[3], (DEPTH, D_MODEL, IN_WIDTH), D_MODEL ** -0.5)
    q_norm = 1.0 + nrm(ks[4], (DEPTH, HEAD_DIM), 0.02)
    k_norm = 1.0 + nrm(ks[5], (DEPTH, HEAD_DIM), 0.02)
    n_idx = jnp.arange(SSM_STATE, dtype=f32)
    lambda_re = -0.5 + nrm(ks[6], (DEPTH, SSM_GROUPS, SSM_STATE), 0.01)
    lambda_im = math.pi * n_idx + nrm(ks[7], (DEPTH, SSM_GROUPS, SSM_STATE), 0.01)
    log_dt = jax.random.uniform(ks[8], (DEPTH, SSM_GROUPS), f32, math.log(DT_MIN), math.log(DT_MAX))
    b_re = nrm(ks[9], (DEPTH, SSM_GROUPS, SSM_STATE, SSM_GROUP), (2.0 * SSM_GROUP) ** -0.5)
    b_im = nrm(ks[10], (DEPTH, SSM_GROUPS, SSM_STATE, SSM_GROUP), (2.0 * SSM_GROUP) ** -0.5)
    c_re = nrm(ks[11], (DEPTH, SSM_GROUPS, SSM_GROUP, SSM_STATE), (2.0 * SSM_STATE) ** -0.5)
    c_im = nrm(ks[12], (DEPTH, SSM_GROUPS, SSM_GROUP, SSM_STATE), (2.0 * SSM_STATE) ** -0.5)
    d_skip = nrm(ks[13], (DEPTH, SSM_WIDTH), 1.0)
    w_glu = nrm(ks[14], (DEPTH, SSM_WIDTH, SSM_WIDTH), SSM_WIDTH ** -0.5)
    b_glu = nrm(ks[15], (DEPTH, SSM_WIDTH), 0.02)
    w_out = nrm(ks[16], (DEPTH, MIX_WIDTH, D_MODEL), MIX_WIDTH ** -0.5)
    ple_norm = 1.0 + nrm(ks[17], (DEPTH, D_MODEL), 0.02)
    w_ple_gate = nrm(ks[18], (DEPTH, D_MODEL, D_MODEL), D_MODEL ** -0.5)
    w_ple_proj = nrm(ks[19], (DEPTH, PLE_DIM, D_MODEL), PLE_DIM ** -0.5)
    return {"x": x, "p": p, "mix_norm": mix_norm, "w_in": w_in, "q_norm": q_norm,
            "k_norm": k_norm, "lambda_re": lambda_re, "lambda_im": lambda_im, "log_dt": log_dt,
            "b_re": b_re, "b_im": b_im, "c_re": c_re, "c_im": c_im, "d_skip": d_skip,
            "w_glu": w_glu, "b_glu": b_glu, "w_out": w_out, "ple_norm": ple_norm,
            "w_ple_gate": w_ple_gate, "w_ple_proj": w_ple_proj}


def reference(x, p, mix_norm, w_in, q_norm, k_norm, lambda_re, lambda_im, log_dt,
              b_re, b_im, c_re, c_im, d_skip, w_glu, b_glu, w_out, ple_norm,
              w_ple_gate, w_ple_proj):
    A = ATTN_WIDTH
    splits = [A, 2 * A, 3 * A, 4 * A, 4 * A + SSM_WIDTH]
    h = x
    for i in range(DEPTH):
        xn = rms_norm(h, mix_norm[i])
        z = xn @ w_in[i]
        q, k, v, gate_a, u, gate_s = jnp.split(z, splits, axis=-1)
        attn = dilated_mixture_attention(q, k, v, q_norm[i], k_norm[i]) * jax.nn.silu(gate_a)
        ssm = s5_glu(u, lambda_re[i], lambda_im[i], log_dt[i], b_re[i], b_im[i], c_re[i],
                     c_im[i], d_skip[i], w_glu[i], b_glu[i]) * jax.nn.silu(gate_s)
        h = h + jnp.concatenate([attn, ssm], axis=-1) @ w_out[i]
        gate = jax.nn.sigmoid(rms_norm(h, ple_norm[i]) @ w_ple_gate[i])
        h = h + gate * (p[i] @ w_ple_proj[i])
    return h
```

```python
import functools
import math

import jax
import jax.numpy as jnp
from jax import lax
from jax.experimental import pallas as pl
from jax.experimental.pallas import tpu as pltpu

F32 = jnp.float32
BF16 = jnp.bfloat16

HEAD_DIM = 64
SSM_GROUP = 16
SSM_STATE = 64
DILATED_CONFIGS = ((128, 1), (512, 4), (2048, 16))
EPS = 1e-6

LANES = 128
QBLK = 128
NEG = -1e30
ROW_TILE = 512
SSM_STEPS = 32
TIME_TILE = 8
GROUPS_PER_CHUNK = LANES // SSM_GROUP
STATE_CHUNK = GROUPS_PER_CHUNK * SSM_STATE
VMEM_LIMIT = 48 * 1024 * 1024


def _sigmoid(x):
    return jax.nn.sigmoid(x)


def _inproj_kernel(x_ref, g_ref, w_ref, q_ref, k_ref, v_ref, ga_ref, u_ref, gs_ref, *, width):
    x = x_ref[0]
    ms = jnp.mean(x * x, axis=-1, keepdims=True)
    xn = (x * lax.rsqrt(ms + EPS) * g_ref[...]).astype(BF16)

    def proj(idx):
        return jnp.dot(xn, w_ref[:, idx * width:(idx + 1) * width], preferred_element_type=F32)

    q_ref[0] = proj(0)
    k_ref[0] = proj(1)
    v_ref[0] = proj(2)
    ga_ref[0] = proj(3)
    u = proj(4)
    for c in range(u.shape[0] // TIME_TILE):
        for j in range(width // LANES):
            u_ref[c, j] = u[c * TIME_TILE:(c + 1) * TIME_TILE, j * LANES:(j + 1) * LANES]
    gs_ref[0] = proj(5)


def _inproj(x, gain, w_bf16):
    b, s, d = x.shape
    width = w_bf16.shape[1] // 6
    nj = width // LANES
    rows = ROW_TILE
    nat = jax.ShapeDtypeStruct((b, s, width), F32)
    nat_spec = pl.BlockSpec((1, rows, width), lambda i, t: (i, t, 0))
    u_shape = jax.ShapeDtypeStruct((s // TIME_TILE, nj, b * TIME_TILE, LANES), F32)
    u_spec = pl.BlockSpec((rows // TIME_TILE, nj, TIME_TILE, LANES), lambda i, t: (t, 0, i, 0))
    return pl.pallas_call(
        functools.partial(_inproj_kernel, width=width),
        grid=(b, s // rows),
        in_specs=[
            pl.BlockSpec((1, rows, d), lambda i, t: (i, t, 0)),
            pl.BlockSpec((1, d), lambda i, t: (0, 0)),
            pl.BlockSpec(w_bf16.shape, lambda i, t: (0, 0)),
        ],
        out_specs=[nat_spec, nat_spec, nat_spec, nat_spec, u_spec, nat_spec],
        out_shape=[nat, nat, nat, nat, u_shape, nat],
        compiler_params=pltpu.CompilerParams(
            dimension_semantics=("arbitrary", "arbitrary"), vmem_limit_bytes=VMEM_LIMIT),
        name="inproj",
    )(x, gain.reshape(1, d), w_bf16)


def _zoh(lam_re, lam_im, dt):
    mag = jnp.exp(lam_re * dt)
    a_re = mag * jnp.cos(lam_im * dt)
    a_im = mag * jnp.sin(lam_im * dt)
    den = lam_re * lam_re + lam_im * lam_im
    num_re = a_re - 1.0
    coef_re = (num_re * lam_re + a_im * lam_im) / den
    coef_im = (a_im * lam_re - num_re * lam_im) / den
    return a_re, a_im, coef_re, coef_im


def _zoh_kernel(lre_ref, lim_ref, ldt_ref, lre_rep_ref, lim_rep_ref, bre_ref, bim_ref,
                are_ref, aim_ref, bbre_ref, bbim_ref):
    dt = jnp.exp(ldt_ref[...])
    a_re, a_im, _, _ = _zoh(lre_ref[...], lim_ref[...], dt)
    are_ref[...] = a_re
    aim_ref[...] = a_im
    _, _, coef_re, coef_im = _zoh(lre_rep_ref[...], lim_rep_ref[...], dt)
    b_re = bre_ref[...]
    b_im = bim_ref[...]
    bbre_ref[...] = coef_re * b_re - coef_im * b_im
    bbim_ref[...] = coef_re * b_im + coef_im * b_re


def _ssm_params(lam_re, lam_im, log_dt, b_re, b_im, c_re, c_im):
    g, n = lam_re.shape
    c = b_re.shape[-1]
    rep = lambda a: jnp.repeat(a, c, axis=1)
    small = jax.ShapeDtypeStruct((g, n), F32)
    wide = jax.ShapeDtypeStruct((g, n * c), F32)
    a_re, a_im, bb_re, bb_im = pl.pallas_call(
        _zoh_kernel, out_shape=[small, small, wide, wide], name="zoh",
    )(lam_re, lam_im, log_dt.reshape(g, 1), rep(lam_re), rep(lam_im),
      b_re.reshape(g, n * c), b_im.reshape(g, n * c))

    nch = g // GROUPS_PER_CHUNK
    eye = jnp.eye(GROUPS_PER_CHUNK, dtype=bool)

    def in_blockdiag(bb):
        t = bb.reshape(nch, GROUPS_PER_CHUNK, n, c).transpose(0, 1, 3, 2)
        m = jnp.where(eye[None, :, None, :, None], t[:, :, :, None, :], 0.0)
        return m.reshape(nch, GROUPS_PER_CHUNK * c, GROUPS_PER_CHUNK * n)

    def out_blockdiag(cc):
        t = cc.reshape(nch, GROUPS_PER_CHUNK, c, n).transpose(0, 1, 3, 2)
        m = jnp.where(eye[None, :, None, :, None], t[:, :, :, None, :], 0.0)
        return m.reshape(nch, GROUPS_PER_CHUNK * n, GROUPS_PER_CHUNK * c)

    w_b = jnp.concatenate([in_blockdiag(bb_re), in_blockdiag(bb_im)], axis=-1).astype(BF16)
    return (a_re.reshape(1, g * n), a_im.reshape(1, g * n), w_b,
            out_blockdiag(c_re).astype(BF16), out_blockdiag(c_im).astype(BF16))


def _attn_kernel(q_ref, k_ref, v_ref, ga_ref, gq_ref, gk_ref, o_ref,
                 qn, kn, ocfg, lcfg, bias_full, bias_first):
    seq = qn.shape[0]
    lane = lax.broadcasted_iota(jnp.int32, (1, LANES), 1)
    head0 = lane < HEAD_DIM
    rr = lax.broadcasted_iota(jnp.int32, (LANES, LANES), 0) < HEAD_DIM
    cc = lax.broadcasted_iota(jnp.int32, (LANES, LANES), 1) < HEAD_DIM
    same_head = (rr == cc).astype(BF16)

    def qk_norm(src_ref, dst, gain, scale):
        def body(i, carry):
            rows = pl.ds(pl.multiple_of(i * 256, 256), 256)
            x = src_ref[0, rows, :]
            x2 = x * x
            hi = x2.astype(BF16)
            lo = (x2 - hi.astype(F32)).astype(BF16)
            ss = (jnp.dot(hi, same_head, preferred_element_type=F32)
                  + jnp.dot(lo, same_head, preferred_element_type=F32))
            y = x * lax.rsqrt(ss * (1.0 / HEAD_DIM) + EPS) * gain
            dst[rows, :] = y * scale if scale != 1.0 else y
            return carry
        lax.fori_loop(0, seq // 256, body, 0)

    qk_norm(q_ref, qn, gq_ref[...], HEAD_DIM ** -0.5)
    qk_norm(k_ref, kn, gk_ref[...], 1.0)

    qi = lax.broadcasted_iota(jnp.int32, (QBLK, 2 * QBLK), 0)
    kj = lax.broadcasted_iota(jnp.int32, (QBLK, 2 * QBLK), 1)
    bias_full[...] = jnp.where((kj >= qi) & (kj <= qi + QBLK), 0.0, NEG)
    qi1 = lax.broadcasted_iota(jnp.int32, (QBLK, QBLK), 0)
    kj1 = lax.broadcasted_iota(jnp.int32, (QBLK, QBLK), 1)
    bias_first[...] = jnp.where(kj1 <= qi1, 0.0, NEG)

    def unit(q_start, dil, cfg, first):
        nk = QBLK if first else 2 * QBLK
        k_start = q_start if first else q_start - QBLK * dil

        def rows(start, n):
            return pl.ds(start, n) if dil == 1 else pl.ds(start, n, stride=dil)

        qb = qn[rows(q_start, QBLK), :]
        kb = kn[rows(k_start, nk), :].astype(BF16)
        vb = v_ref.at[0][rows(k_start, nk), :]
        bias = bias_first[...] if first else bias_full[...]
        outs, lses = [], []
        for h in range(2):
            sel = head0 if h == 0 else jnp.logical_not(head0)
            qh = jnp.where(sel, qb, 0.0).astype(BF16)
            s = lax.dot_general(qh, kb, (((1,), (1,)), ((), ())), preferred_element_type=F32) + bias
            m = jnp.max(s, axis=-1, keepdims=True)
            e = jnp.exp(s - m).astype(BF16)
            vh = jnp.where(sel, vb, 1.0).astype(BF16)
            o = jnp.dot(e, vh, preferred_element_type=F32)
            den = pltpu.roll(o, HEAD_DIM, 1)
            outs.append(o / den)
            lses.append(m + jnp.log(den))
        ocfg.at[cfg][rows(q_start, QBLK), :] = jnp.where(head0, outs[0], outs[1])
        lcfg.at[cfg][rows(q_start, QBLK), :] = jnp.where(head0, lses[0], lses[1])

    for cfg, (window, dil) in enumerate(DILATED_CONFIGS):
        assert window // dil == QBLK
        blocks = seq // (dil * QBLK)

        def per_class(r, carry, dil=dil, cfg=cfg, blocks=blocks):
            unit(r, dil, cfg, True)
            if blocks > 1:
                def per_block(i, c2):
                    start = r + i * (dil * QBLK)
                    if dil == 1:
                        start = pl.multiple_of(start, QBLK)
                    unit(start, dil, cfg, False)
                    return c2
                lax.fori_loop(1, blocks, per_block, 0)
            return carry

        if dil == 1:
            per_class(0, 0)
        else:
            lax.fori_loop(0, dil, per_class, 0)

    def merge(i, carry):
        rows = pl.ds(pl.multiple_of(i * QBLK, QBLK), QBLK)
        ls = [lcfg[c, rows, :] for c in range(len(DILATED_CONFIGS))]
        mx = functools.reduce(jnp.maximum, ls)
        ws = [jnp.exp(l - mx) for l in ls]
        num = functools.reduce(jnp.add, [w * ocfg[c, rows, :] for c, w in enumerate(ws)])
        o = num / functools.reduce(jnp.add, ws)
        g = ga_ref[0, rows, :]
        o_ref[0, rows, :] = (o * (g * _sigmoid(g))).astype(o_ref.dtype)
        return carry

    lax.fori_loop(0, seq // QBLK, merge, 0)


def _attention(q, k, v, gate, q_gain, k_gain):
    b, s, width = q.shape
    pairs = width // LANES
    ncfg = len(DILATED_CONFIGS)
    spec = pl.BlockSpec((1, s, LANES), lambda i, h: (i, 0, h))
    gspec = pl.BlockSpec((1, LANES), lambda i, h: (0, 0))
    tile2 = lambda g: jnp.concatenate([g, g]).reshape(1, LANES)
    return pl.pallas_call(
        _attn_kernel,
        grid=(b, pairs),
        in_specs=[spec, spec, spec, spec, gspec, gspec],
        out_specs=spec,
        out_shape=jax.ShapeDtypeStruct((b, s, width), BF16),
        scratch_shapes=[
            pltpu.VMEM((s, LANES), F32), pltpu.VMEM((s, LANES), F32),
            pltpu.VMEM((ncfg, s, LANES), F32), pltpu.VMEM((ncfg, s, LANES), F32),
            pltpu.VMEM((QBLK, 2 * QBLK), F32), pltpu.VMEM((QBLK, QBLK), F32),
        ],
        compiler_params=pltpu.CompilerParams(
            dimension_semantics=("arbitrary", "arbitrary"), vmem_limit_bytes=VMEM_LIMIT),
        name="attn",
    )(q, k, v, gate, tile2(q_gain), tile2(k_gain))


def _ssm_kernel(u_ref, wb_ref, wcr_ref, wci_ref, are_ref, aim_ref, d_ref, y_ref,
                ut, xre, xim, sre, sim, *, batch):
    nslab, nj = u_ref.shape[0], u_ref.shape[1]
    steps = nslab * TIME_TILE

    @pl.when(pl.program_id(0) == 0)
    def _():
        sre[...] = jnp.zeros_like(sre)
        sim[...] = jnp.zeros_like(sim)

    for c in range(nslab):
        for j in range(nj):
            src = u_ref.at[c, j]
            for t in range(TIME_TILE):
                ut[pl.ds((c * TIME_TILE + t) * batch, batch), j * LANES:(j + 1) * LANES] = (
                    src[pl.ds(t, batch, stride=TIME_TILE), :])

    for j in range(nj):
        sl = slice(j * STATE_CHUNK, (j + 1) * STATE_CHUNK)
        bu = jnp.dot(ut[:, j * LANES:(j + 1) * LANES].astype(BF16), wb_ref[j],
                     preferred_element_type=F32)
        xre[:, sl] = bu[:, :STATE_CHUNK]
        xim[:, sl] = bu[:, STATE_CHUNK:]

    for j in range(nj):
        sl = slice(j * STATE_CHUNK, (j + 1) * STATE_CHUNK)
        a_re = jnp.broadcast_to(are_ref[:, sl], (batch, STATE_CHUNK))
        a_im = jnp.broadcast_to(aim_ref[:, sl], (batch, STATE_CHUNK))

        def step(t, carry, sl=sl, a_re=a_re, a_im=a_im):
            x_re, x_im = carry
            rows = pl.ds(pl.multiple_of(t * batch, batch), batch)
            n_re = a_re * x_re - a_im * x_im + xre[rows, sl]
            n_im = a_re * x_im + a_im * x_re + xim[rows, sl]
            xre[rows, sl] = n_re
            xim[rows, sl] = n_im
            return n_re, n_im

        f_re, f_im = lax.fori_loop(0, steps, step, (sre[:, sl], sim[:, sl]), unroll=8)
        sre[:, sl] = f_re
        sim[:, sl] = f_im

    sqrt_half = math.sqrt(0.5)
    for j in range(nj):
        sl = slice(j * STATE_CHUNK, (j + 1) * STATE_CHUNK)
        lanes = slice(j * LANES, (j + 1) * LANES)
        y = (jnp.dot(xre[:, sl].astype(BF16), wcr_ref[j], preferred_element_type=F32)
             - jnp.dot(xim[:, sl].astype(BF16), wci_ref[j], preferred_element_type=F32))
        y = y + d_ref[:, lanes] * ut[:, lanes]
        yg = 0.5 * y * (1.0 + lax.erf(y * sqrt_half))
        for c in range(nslab):
            dst = y_ref.at[c, j]
            for t in range(TIME_TILE):
                r0 = (c * TIME_TILE + t) * batch
                dst[pl.ds(t, batch, stride=TIME_TILE), :] = yg[r0:r0 + batch, :]


def _ssm(u_tiles, batch, w_b, w_cre, w_cim, a_re, a_im, d_skip):
    nslab_total, nj, _, _ = u_tiles.shape
    nslab = SSM_STEPS // TIME_TILE
    rows = SSM_STEPS * batch
    nstate = a_re.shape[1]
    tile_spec = pl.BlockSpec((nslab, nj, batch * TIME_TILE, LANES), lambda t: (t, 0, 0, 0))
    full = lambda a: pl.BlockSpec(a.shape, lambda t: (0,) * a.ndim)
    d2 = d_skip.reshape(1, -1)
    return pl.pallas_call(
        functools.partial(_ssm_kernel, batch=batch),
        grid=(nslab_total // nslab,),
        in_specs=[tile_spec, full(w_b), full(w_cre), full(w_cim), full(a_re), full(a_im), full(d2)],
        out_specs=tile_spec,
        out_shape=jax.ShapeDtypeStruct(u_tiles.shape, F32),
        scratch_shapes=[
            pltpu.VMEM((rows, nj * LANES), F32),
            pltpu.VMEM((rows, nstate), F32), pltpu.VMEM((rows, nstate), F32),
            pltpu.VMEM((batch, nstate), F32), pltpu.VMEM((batch, nstate), F32),
        ],
        compiler_params=pltpu.CompilerParams(
            dimension_semantics=("arbitrary",), vmem_limit_bytes=VMEM_LIMIT),
        name="ssm",
    )(u_tiles, w_b, w_cre, w_cim, a_re, a_im, d2)


def _out_kernel(attn_ref, y_ref, gs_ref, x_ref, p_ref, wglu_ref, bglu_ref, wout_ref, gple_ref,
                wg_ref, wp_ref, o_ref):
    nslab, nj = y_ref.shape[0], y_ref.shape[1]
    rows = nslab * TIME_TILE
    aw = attn_ref.shape[-1]
    yg = jnp.concatenate([y_ref[:, j].reshape(rows, LANES) for j in range(nj)], axis=1)
    glu = jnp.dot(yg.astype(BF16), wglu_ref[...], preferred_element_type=F32) + bglu_ref[...]
    gs = gs_ref[0]
    ssm = yg * _sigmoid(glu) * (gs * _sigmoid(gs))
    h = (x_ref[0]
         + jnp.dot(attn_ref[0], wout_ref[:aw, :], preferred_element_type=F32)
         + jnp.dot(ssm.astype(BF16), wout_ref[aw:, :], preferred_element_type=F32))
    ms = jnp.mean(h * h, axis=-1, keepdims=True)
    hn = (h * lax.rsqrt(ms + EPS) * gple_ref[...]).astype(BF16)
    gate = _sigmoid(jnp.dot(hn, wg_ref[...], preferred_element_type=F32))
    pp = jnp.dot(p_ref[0].astype(BF16), wp_ref[...], preferred_element_type=F32)
    o_ref[0] = h + gate * pp


def _out_stage(attn, y_tiles, gate_s, x, p, w_glu, b_glu, w_out, ple_gain, w_gate, w_proj):
    b, s, d = x.shape
    rows = ROW_TILE
    nj = y_tiles.shape[1]
    nat = lambda a: pl.BlockSpec((1, rows, a.shape[-1]), lambda i, t: (i, t, 0))
    full = lambda a: pl.BlockSpec(a.shape, lambda i, t: (0,) * a.ndim)
    y_spec = pl.BlockSpec((rows // TIME_TILE, nj, TIME_TILE, LANES), lambda i, t: (t, 0, i, 0))
    b2 = b_glu.reshape(1, -1)
    g2 = ple_gain.reshape(1, -1)
    return pl.pallas_call(
        _out_kernel,
        grid=(b, s // rows),
        in_specs=[nat(attn), y_spec, nat(gate_s), nat(x), nat(p), full(w_glu), full(b2), full(w_out),
                  full(g2), full(w_gate), full(w_proj)],
        out_specs=nat(x),
        out_shape=jax.ShapeDtypeStruct(x.shape, x.dtype),
        compiler_params=pltpu.CompilerParams(
            dimension_semantics=("arbitrary", "arbitrary"), vmem_limit_bytes=VMEM_LIMIT),
        name="outstage",
    )(attn, y_tiles, gate_s, x, p, w_glu, b2, w_out, g2, w_gate, w_proj)


def kernel(x, p, mix_norm, w_in, q_norm, k_norm, lambda_re, lambda_im, log_dt, b_re, b_im, c_re, c_im,
           d_skip, w_glu, b_glu, w_out, ple_norm, w_ple_gate, w_ple_proj):
    batch = x.shape[0]
    h = x
    for i in range(w_in.shape[0]):
        q, k, v, gate_a, u_tiles, gate_s = _inproj(h, mix_norm[i], w_in[i].astype(BF16))
        a_re, a_im, w_b, w_cre, w_cim = _ssm_params(
            lambda_re[i], lambda_im[i], log_dt[i], b_re[i], b_im[i], c_re[i], c_im[i])
        attn = _attention(q, k, v, gate_a, q_norm[i], k_norm[i])
        y_tiles = _ssm(u_tiles, batch, w_b, w_cre, w_cim, a_re, a_im, d_skip[i])
        h = _out_stage(attn, y_tiles, gate_s, h, p[i], w_glu[i].astype(BF16), b_glu[i],
                       w_out[i].astype(BF16), ple_norm[i], w_ple_gate[i].astype(BF16),
                       w_ple_proj[i].astype(BF16))
    return h
```

```python
import functools
import math

import jax
import jax.numpy as jnp
from jax import lax
from jax.experimental import pallas as pl
from jax.experimental.pallas import tpu as pltpu

F32 = jnp.float32
BF16 = jnp.bfloat16

HEAD_DIM = 64
SSM_GROUP = 16
SSM_STATE = 64
DILATED_CONFIGS = ((128, 1), (512, 4), (2048, 16))
EPS = 1e-6

LANES = 128
QBLK = 128
NEG = -1e30
LOG2E = math.log2(math.e)
ROW_TILE = 512
SSM_STEPS = 32
TIME_TILE = 8
GROUPS_PER_CHUNK = LANES // SSM_GROUP
STATE_CHUNK = GROUPS_PER_CHUNK * SSM_STATE
VMEM_LIMIT = 48 * 1024 * 1024


def _sigmoid(x):
    return jax.nn.sigmoid(x)


def _inproj_kernel(x_ref, g_ref, w_ref, q_ref, k_ref, v_ref, ga_ref, u_ref, gs_ref, *, width):
    x = x_ref[0]
    ms = jnp.mean(x * x, axis=-1, keepdims=True)
    xn = (x * lax.rsqrt(ms + EPS) * g_ref[...]).astype(BF16)

    def proj(idx):
        return jnp.dot(xn, w_ref[:, idx * width:(idx + 1) * width], preferred_element_type=F32)

    q_ref[0] = proj(0)
    k_ref[0] = proj(1)
    v_ref[0] = proj(2)
    ga_ref[0] = proj(3)
    u = proj(4)
    for c in range(u.shape[0] // TIME_TILE):
        for j in range(width // LANES):
            u_ref[c, j] = u[c * TIME_TILE:(c + 1) * TIME_TILE, j * LANES:(j + 1) * LANES]
    gs_ref[0] = proj(5)


def _inproj(x, gain, w_bf16):
    b, s, d = x.shape
    width = w_bf16.shape[1] // 6
    nj = width // LANES
    rows = ROW_TILE
    nat = jax.ShapeDtypeStruct((b, s, width), F32)
    nat_spec = pl.BlockSpec((1, rows, width), lambda i, t: (i, t, 0))
    u_shape = jax.ShapeDtypeStruct((s // TIME_TILE, nj, b * TIME_TILE, LANES), F32)
    u_spec = pl.BlockSpec((rows // TIME_TILE, nj, TIME_TILE, LANES), lambda i, t: (t, 0, i, 0))
    return pl.pallas_call(
        functools.partial(_inproj_kernel, width=width),
        grid=(b, s // rows),
        in_specs=[
            pl.BlockSpec((1, rows, d), lambda i, t: (i, t, 0)),
            pl.BlockSpec((1, d), lambda i, t: (0, 0)),
            pl.BlockSpec(w_bf16.shape, lambda i, t: (0, 0)),
        ],
        out_specs=[nat_spec, nat_spec, nat_spec, nat_spec, u_spec, nat_spec],
        out_shape=[nat, nat, nat, nat, u_shape, nat],
        compiler_params=pltpu.CompilerParams(
            dimension_semantics=("arbitrary", "arbitrary"), vmem_limit_bytes=VMEM_LIMIT),
        name="inproj",
    )(x, gain.reshape(1, d), w_bf16)


def _zoh(lam_re, lam_im, dt):
    mag = jnp.exp(lam_re * dt)
    a_re = mag * jnp.cos(lam_im * dt)
    a_im = mag * jnp.sin(lam_im * dt)
    den = lam_re * lam_re + lam_im * lam_im
    num_re = a_re - 1.0
    coef_re = (num_re * lam_re + a_im * lam_im) / den
    coef_im = (a_im * lam_re - num_re * lam_im) / den
    return a_re, a_im, coef_re, coef_im


def _zoh_kernel(lre_ref, lim_ref, ldt_ref, lre_rep_ref, lim_rep_ref, bre_ref, bim_ref,
                are_ref, aim_ref, bbre_ref, bbim_ref):
    dt = jnp.exp(ldt_ref[...])
    a_re, a_im, _, _ = _zoh(lre_ref[...], lim_ref[...], dt)
    are_ref[...] = a_re
    aim_ref[...] = a_im
    _, _, coef_re, coef_im = _zoh(lre_rep_ref[...], lim_rep_ref[...], dt)
    b_re = bre_ref[...]
    b_im = bim_ref[...]
    bbre_ref[...] = coef_re * b_re - coef_im * b_im
    bbim_ref[...] = coef_re * b_im + coef_im * b_re


def _ssm_params(lam_re, lam_im, log_dt, b_re, b_im, c_re, c_im):
    g, n = lam_re.shape
    c = b_re.shape[-1]
    rep = lambda a: jnp.repeat(a, c, axis=1)
    small = jax.ShapeDtypeStruct((g, n), F32)
    wide = jax.ShapeDtypeStruct((g, n * c), F32)
    a_re, a_im, bb_re, bb_im = pl.pallas_call(
        _zoh_kernel, out_shape=[small, small, wide, wide], name="zoh",
    )(lam_re, lam_im, log_dt.reshape(g, 1), rep(lam_re), rep(lam_im),
      b_re.reshape(g, n * c), b_im.reshape(g, n * c))

    nch = g // GROUPS_PER_CHUNK
    eye = jnp.eye(GROUPS_PER_CHUNK, dtype=bool)

    def in_blockdiag(bb):
        t = bb.reshape(nch, GROUPS_PER_CHUNK, n, c).transpose(0, 1, 3, 2)
        m = jnp.where(eye[None, :, None, :, None], t[:, :, :, None, :], 0.0)
        return m.reshape(nch, GROUPS_PER_CHUNK * c, GROUPS_PER_CHUNK * n)

    def out_blockdiag(cc):
        t = cc.reshape(nch, GROUPS_PER_CHUNK, c, n).transpose(0, 1, 3, 2)
        m = jnp.where(eye[None, :, None, :, None], t[:, :, :, None, :], 0.0)
        return m.reshape(nch, GROUPS_PER_CHUNK * n, GROUPS_PER_CHUNK * c)

    w_b = jnp.concatenate([in_blockdiag(bb_re), in_blockdiag(bb_im)], axis=-1).astype(BF16)
    return (a_re.reshape(1, g * n), a_im.reshape(1, g * n), w_b,
            out_blockdiag(c_re).astype(BF16), out_blockdiag(c_im).astype(BF16))


def _attn_kernel(q_ref, k_ref, v_ref, ga_ref, gq_ref, gk_ref, o_ref,
                 qn, kn, x4, ops, res, bias_full, bias_first):
    seq = qn.shape[0]
    nblk = seq // QBLK
    lane = lax.broadcasted_iota(jnp.int32, (1, LANES), 1)
    head0 = lane < HEAD_DIM
    rr = lax.broadcasted_iota(jnp.int32, (LANES, LANES), 0) < HEAD_DIM
    cc = lax.broadcasted_iota(jnp.int32, (LANES, LANES), 1) < HEAD_DIM
    same_head = (rr == cc).astype(BF16)
    gq = gq_ref[...]
    gk = gk_ref[...]

    def blk(i, n=QBLK):
        start = i * QBLK
        return pl.ds(start if isinstance(i, int) else pl.multiple_of(start, QBLK), n)

    def write_operands(rows, qv, kv, vv):
        ops[0, rows, :] = jnp.where(head0, qv, 0.0).astype(BF16)
        ops[1, rows, :] = jnp.where(head0, 0.0, qv).astype(BF16)
        ops[2, rows, :] = kv.astype(BF16)
        ops[3, rows, :] = jnp.where(head0, vv, 0.0).astype(BF16)
        ops[4, rows, :] = jnp.where(head0, 0.0, vv).astype(BF16)

    def rms(x, gain):
        x2 = x * x
        hi = x2.astype(BF16)
        lo = (x2 - hi.astype(F32)).astype(BF16)
        ss = (jnp.dot(hi, same_head, preferred_element_type=F32)
              + jnp.dot(lo, same_head, preferred_element_type=F32))
        return x * lax.rsqrt(ss * (1.0 / HEAD_DIM) + EPS) * gain

    def norm_block(i, carry):
        rows = blk(i)
        qv = rms(q_ref[0, rows, :], gq) * (HEAD_DIM ** -0.5 * LOG2E)
        kv = rms(k_ref[0, rows, :], gk)
        qn[rows, :] = qv
        kn[rows, :] = kv
        write_operands(rows, qv, kv, v_ref[0, rows, :])
        return carry

    def to_mod4(c, carry):
        src = pl.ds((c >> 2) + (c & 3) * (4 * QBLK), QBLK, stride=4)
        rows = blk(c)
        qv, kv, vv = qn[src, :], kn[src, :], v_ref.at[0][src, :]
        x4[0, rows, :] = qv
        x4[1, rows, :] = kv
        x4[2, rows, :] = vv
        write_operands(rows, qv, kv, vv)
        return carry

    def to_mod16(c, carry):
        src = pl.ds((c >> 2) * (4 * QBLK) + (c & 3), QBLK, stride=4)
        write_operands(blk(c), x4.at[0][src, :], x4.at[1][src, :], x4.at[2][src, :])
        return carry

    qi = lax.broadcasted_iota(jnp.int32, (QBLK, 2 * QBLK), 0)
    kj = lax.broadcasted_iota(jnp.int32, (QBLK, 2 * QBLK), 1)
    bias_full[...] = jnp.where((kj >= qi) & (kj <= qi + QBLK), 0.0, NEG)
    qi1 = lax.broadcasted_iota(jnp.int32, (QBLK, QBLK), 0)
    kj1 = lax.broadcasted_iota(jnp.int32, (QBLK, QBLK), 1)
    bias_first[...] = jnp.where(kj1 <= qi1, 0.0, NEG)

    one0 = jnp.where(head0, 1.0, 0.0).astype(BF16)
    one1 = jnp.where(head0, 0.0, 1.0).astype(BF16)

    def unit(b, first, cfg, dst):
        nk = QBLK if first else 2 * QBLK
        qrows = blk(b)
        krows = qrows if first else blk(b - 1, nk)
        ones0 = jnp.broadcast_to(one0, (nk, LANES))
        ones1 = jnp.broadcast_to(one1, (nk, LANES))
        kb = ops[2, krows, :]
        bias = bias_first[...] if first else bias_full[...]
        es, ms = [], []
        for h in range(2):
            s = lax.dot_general(ops[h, qrows, :], kb, (((1,), (1,)), ((), ())),
                                preferred_element_type=F32) + bias
            m = jnp.max(s, axis=-1, keepdims=True)
            es.append(jnp.exp2(s - m).astype(BF16))
            ms.append(m)
        rhs = jnp.concatenate([
            jnp.concatenate([ops[3, krows, :], ones0], axis=1),
            jnp.concatenate([ops[4, krows, :], ones1], axis=1)], axis=0)
        o = jnp.dot(jnp.concatenate(es, axis=1), rhs, preferred_element_type=F32)
        res.at[cfg, 0][dst, :] = jnp.where(head0, ms[0], ms[1])
        res.at[cfg, 1][dst, :] = o[:, :LANES]
        res.at[cfg, 2][dst, :] = o[:, LANES:]

    def class_of_four(base, cfg):
        unit(base, True, cfg, blk(base))
        for i in range(1, 4):
            unit(base + i, False, cfg, blk(base + i))

    def four_full(g, carry):
        for i in range(4):
            unit(4 * g + i, False, 0, blk(4 * g + i))
        return carry

    def eight_first(g, carry):
        for i in range(8):
            dst = pl.ds((2 * g + i // 4) * (4 * QBLK) + i % 4, QBLK, stride=4)
            unit(8 * g + i, True, 2, dst)
        return carry

    assert DILATED_CONFIGS == ((128, 1), (512, 4), (2048, 16)) and nblk == 16
    lax.fori_loop(0, nblk, norm_block, 0, unroll=4)
    class_of_four(0, 0)
    lax.fori_loop(1, nblk // 4, four_full, 0)
    lax.fori_loop(0, nblk, to_mod4, 0, unroll=4)
    lax.fori_loop(0, 4, lambda r, c: (class_of_four(4 * r, 1), c)[1], 0)
    lax.fori_loop(0, nblk, to_mod16, 0, unroll=4)
    lax.fori_loop(0, nblk // 8, eight_first, 0)

    def combine(parts):
        mx = functools.reduce(jnp.maximum, [p[0] for p in parts])
        ws = [jnp.exp2(p[0] - mx) for p in parts]
        num = functools.reduce(jnp.add, [w * p[1] for w, p in zip(ws, parts)])
        den = functools.reduce(jnp.add, [w * p[2] for w, p in zip(ws, parts)])
        return mx, num, den

    def merge_mod4(c, carry):
        rows = blk(c)
        dst = pl.ds((c >> 2) + (c & 3) * (4 * QBLK), QBLK, stride=4)
        parts = [[res[cfg, k, rows, :] for k in range(3)] for cfg in (1, 2)]
        for k, val in enumerate(combine(parts)):
            x4.at[k][dst, :] = val
        return carry

    def merge_out(i, carry):
        rows = blk(i)
        parts = [[res[0, k, rows, :] for k in range(3)], [x4[k, rows, :] for k in range(3)]]
        _, num, den = combine(parts)
        g = ga_ref[0, rows, :]
        o_ref[0, rows, :] = (num / den * (g * _sigmoid(g))).astype(o_ref.dtype)
        return carry

    lax.fori_loop(0, nblk, merge_mod4, 0, unroll=2)
    lax.fori_loop(0, nblk, merge_out, 0, unroll=2)


def _attention(q, k, v, gate, q_gain, k_gain):
    b, s, width = q.shape
    pairs = width // LANES
    ncfg = len(DILATED_CONFIGS)
    spec = pl.BlockSpec((1, s, LANES), lambda i, h: (i, 0, h))
    gspec = pl.BlockSpec((1, LANES), lambda i, h: (0, 0))
    tile2 = lambda g: jnp.concatenate([g, g]).reshape(1, LANES)
    return pl.pallas_call(
        _attn_kernel,
        grid=(b, pairs),
        in_specs=[spec, spec, spec, spec, gspec, gspec],
        out_specs=spec,
        out_shape=jax.ShapeDtypeStruct((b, s, width), BF16),
        scratch_shapes=[
            pltpu.VMEM((s, LANES), F32), pltpu.VMEM((s, LANES), F32),
            pltpu.VMEM((3, s, LANES), F32),
            pltpu.VMEM((5, s, LANES), BF16),
            pltpu.VMEM((ncfg, 3, s, LANES), F32),
            pltpu.VMEM((QBLK, 2 * QBLK), F32), pltpu.VMEM((QBLK, QBLK), F32),
        ],
        compiler_params=pltpu.CompilerParams(
            dimension_semantics=("arbitrary", "arbitrary"), vmem_limit_bytes=VMEM_LIMIT),
        name="attn",
    )(q, k, v, gate, tile2(q_gain), tile2(k_gain))


def _ssm_kernel(u_ref, wb_ref, wcr_ref, wci_ref, are_ref, aim_ref, d_ref, y_ref,
                ut, xre, xim, sre, sim, *, batch):
    nslab, nj = u_ref.shape[0], u_ref.shape[1]
    steps = nslab * TIME_TILE

    @pl.when(pl.program_id(0) == 0)
    def _():
        sre[...] = jnp.zeros_like(sre)
        sim[...] = jnp.zeros_like(sim)

    for c in range(nslab):
        for j in range(nj):
            src = u_ref.at[c, j]
            for t in range(TIME_TILE):
                ut[pl.ds((c * TIME_TILE + t) * batch, batch), j * LANES:(j + 1) * LANES] = (
                    src[pl.ds(t, batch, stride=TIME_TILE), :])

    for j in range(nj):
        sl = slice(j * STATE_CHUNK, (j + 1) * STATE_CHUNK)
        bu = jnp.dot(ut[:, j * LANES:(j + 1) * LANES].astype(BF16), wb_ref[j],
                     preferred_element_type=F32)
        xre[:, sl] = bu[:, :STATE_CHUNK]
        xim[:, sl] = bu[:, STATE_CHUNK:]

    for j in range(nj):
        sl = slice(j * STATE_CHUNK, (j + 1) * STATE_CHUNK)
        a_re = jnp.broadcast_to(are_ref[:, sl], (batch, STATE_CHUNK))
        a_im = jnp.broadcast_to(aim_ref[:, sl], (batch, STATE_CHUNK))

        def step(t, carry, sl=sl, a_re=a_re, a_im=a_im):
            x_re, x_im = carry
            rows = pl.ds(pl.multiple_of(t * batch, batch), batch)
            n_re = a_re * x_re - a_im * x_im + xre[rows, sl]
            n_im = a_re * x_im + a_im * x_re + xim[rows, sl]
            xre[rows, sl] = n_re
            xim[rows, sl] = n_im
            return n_re, n_im

        f_re, f_im = lax.fori_loop(0, steps, step, (sre[:, sl], sim[:, sl]), unroll=8)
        sre[:, sl] = f_re
        sim[:, sl] = f_im

    sqrt_half = math.sqrt(0.5)
    for j in range(nj):
        sl = slice(j * STATE_CHUNK, (j + 1) * STATE_CHUNK)
        lanes = slice(j * LANES, (j + 1) * LANES)
        y = (jnp.dot(xre[:, sl].astype(BF16), wcr_ref[j], preferred_element_type=F32)
             - jnp.dot(xim[:, sl].astype(BF16), wci_ref[j], preferred_element_type=F32))
        y = y + d_ref[:, lanes] * ut[:, lanes]
        yg = 0.5 * y * (1.0 + lax.erf(y * sqrt_half))
        for c in range(nslab):
            dst = y_ref.at[c, j]
            for t in range(TIME_TILE):
                r0 = (c * TIME_TILE + t) * batch
                dst[pl.ds(t, batch, stride=TIME_TILE), :] = yg[r0:r0 + batch, :]


def _ssm(u_tiles, batch, w_b, w_cre, w_cim, a_re, a_im, d_skip):
    nslab_total, nj, _, _ = u_tiles.shape
    nslab = SSM_STEPS // TIME_TILE
    rows = SSM_STEPS * batch
    nstate = a_re.shape[1]
    tile_spec = pl.BlockSpec((nslab, nj, batch * TIME_TILE, LANES), lambda t: (t, 0, 0, 0))
    full = lambda a: pl.BlockSpec(a.shape, lambda t: (0,) * a.ndim)
    d2 = d_skip.reshape(1, -1)
    return pl.pallas_call(
        functools.partial(_ssm_kernel, batch=batch),
        grid=(nslab_total // nslab,),
        in_specs=[tile_spec, full(w_b), full(w_cre), full(w_cim), full(a_re), full(a_im), full(d2)],
        out_specs=tile_spec,
        out_shape=jax.ShapeDtypeStruct(u_tiles.shape, F32),
        scratch_shapes=[
            pltpu.VMEM((rows, nj * LANES), F32),
            pltpu.VMEM((rows, nstate), F32), pltpu.VMEM((rows, nstate), F32),
            pltpu.VMEM((batch, nstate), F32), pltpu.VMEM((batch, nstate), F32),
        ],
        compiler_params=pltpu.CompilerParams(
            dimension_semantics=("arbitrary",), vmem_limit_bytes=VMEM_LIMIT),
        name="ssm",
    )(u_tiles, w_b, w_cre, w_cim, a_re, a_im, d2)


def _out_kernel(attn_ref, y_ref, gs_ref, x_ref, p_ref, wglu_ref, bglu_ref, wout_ref, gple_ref,
                wg_ref, wp_ref, o_ref):
    nslab, nj = y_ref.shape[0], y_ref.shape[1]
    rows = nslab * TIME_TILE
    aw = attn_ref.shape[-1]
    yg = jnp.concatenate([y_ref[:, j].reshape(rows, LANES) for j in range(nj)], axis=1)
    glu = jnp.dot(yg.astype(BF16), wglu_ref[...], preferred_element_type=F32) + bglu_ref[...]
    gs = gs_ref[0]
    ssm = yg * _sigmoid(glu) * (gs * _sigmoid(gs))
    h = (x_ref[0]
         + jnp.dot(attn_ref[0], wout_ref[:aw, :], preferred_element_type=F32)
         + jnp.dot(ssm.astype(BF16), wout_ref[aw:, :], preferred_element_type=F32))
    ms = jnp.mean(h * h, axis=-1, keepdims=True)
    hn = (h * lax.rsqrt(ms + EPS) * gple_ref[...]).astype(BF16)
    gate = _sigmoid(jnp.dot(hn, wg_ref[...], preferred_element_type=F32))
    pp = jnp.dot(p_ref[0].astype(BF16), wp_ref[...], preferred_element_type=F32)
    o_ref[0] = h + gate * pp


def _out_stage(attn, y_tiles, gate_s, x, p, w_glu, b_glu, w_out, ple_gain, w_gate, w_proj):
    b, s, d = x.shape
    rows = ROW_TILE
    nj = y_tiles.shape[1]
    nat = lambda a: pl.BlockSpec((1, rows, a.shape[-1]), lambda i, t: (i, t, 0))
    full = lambda a: pl.BlockSpec(a.shape, lambda i, t: (0,) * a.ndim)
    y_spec = pl.BlockSpec((rows // TIME_TILE, nj, TIME_TILE, LANES), lambda i, t: (t, 0, i, 0))
    b2 = b_glu.reshape(1, -1)
    g2 = ple_gain.reshape(1, -1)
    return pl.pallas_call(
        _out_kernel,
        grid=(b, s // rows),
        in_specs=[nat(attn), y_spec, nat(gate_s), nat(x), nat(p), full(w_glu), full(b2), full(w_out),
                  full(g2), full(w_gate), full(w_proj)],
        out_specs=nat(x),
        out_shape=jax.ShapeDtypeStruct(x.shape, x.dtype),
        compiler_params=pltpu.CompilerParams(
            dimension_semantics=("arbitrary", "arbitrary"), vmem_limit_bytes=VMEM_LIMIT),
        name="outstage",
    )(attn, y_tiles, gate_s, x, p, w_glu, b2, w_out, g2, w_gate, w_proj)


def kernel(x, p, mix_norm, w_in, q_norm, k_norm, lambda_re, lambda_im, log_dt, b_re, b_im, c_re, c_im,
           d_skip, w_glu, b_glu, w_out, ple_norm, w_ple_gate, w_ple_proj):
    batch = x.shape[0]
    h = x
    for i in range(w_in.shape[0]):
        q, k, v, gate_a, u_tiles, gate_s = _inproj(h, mix_norm[i], w_in[i].astype(BF16))
        a_re, a_im, w_b, w_cre, w_cim = _ssm_params(
            lambda_re[i], lambda_im[i], log_dt[i], b_re[i], b_im[i], c_re[i], c_im[i])
        attn = _attention(q, k, v, gate_a, q_norm[i], k_norm[i])
        y_tiles = _ssm(u_tiles, batch, w_b, w_cre, w_cim, a_re, a_im, d_skip[i])
        h = _out_stage(attn, y_tiles, gate_s, h, p[i], w_glu[i].astype(BF16), b_glu[i],
                       w_out[i].astype(BF16), ple_norm[i], w_ple_gate[i].astype(BF16),
                       w_ple_proj[i].astype(BF16))
    return h
```

```python
import functools
import math

import jax
import jax.numpy as jnp
from jax import lax
from jax.experimental import pallas as pl
from jax.experimental.pallas import tpu as pltpu

F32 = jnp.float32
BF16 = jnp.bfloat16

HEAD_DIM = 64
SSM_GROUP = 16
SSM_STATE = 64
DILATED_CONFIGS = ((128, 1), (512, 4), (2048, 16))
EPS = 1e-6

LANES = 128
QBLK = 128
NEG = -1e30
LOG2E = math.log2(math.e)
ROW_TILE = 512
SSM_STEPS = 32
TIME_TILE = 8
GROUPS_PER_CHUNK = LANES // SSM_GROUP
STATE_CHUNK = GROUPS_PER_CHUNK * SSM_STATE
VMEM_LIMIT = 48 * 1024 * 1024


def _sigmoid(x):
    return jax.nn.sigmoid(x)


def _inproj_kernel(x_ref, g_ref, w_ref, q_ref, k_ref, v_ref, ga_ref, u_ref, gs_ref, *, width):
    x = x_ref[0]
    ms = jnp.mean(x * x, axis=-1, keepdims=True)
    xn = (x * lax.rsqrt(ms + EPS) * g_ref[...]).astype(BF16)

    def proj(idx):
        return jnp.dot(xn, w_ref[:, idx * width:(idx + 1) * width], preferred_element_type=F32)

    q_ref[0] = proj(0)
    k_ref[0] = proj(1)
    v_ref[0] = proj(2)
    ga_ref[0] = proj(3)
    u = proj(4)
    for c in range(u.shape[0] // TIME_TILE):
        for j in range(width // LANES):
            u_ref[c, j] = u[c * TIME_TILE:(c + 1) * TIME_TILE, j * LANES:(j + 1) * LANES]
    gs_ref[0] = proj(5)


def _inproj(x, gain, w_bf16):
    b, s, d = x.shape
    width = w_bf16.shape[1] // 6
    nj = width // LANES
    rows = ROW_TILE
    nat = jax.ShapeDtypeStruct((b, s, width), F32)
    nat_spec = pl.BlockSpec((1, rows, width), lambda i, t: (i, t, 0))
    u_shape = jax.ShapeDtypeStruct((s // TIME_TILE, nj, b * TIME_TILE, LANES), F32)
    u_spec = pl.BlockSpec((rows // TIME_TILE, nj, TIME_TILE, LANES), lambda i, t: (t, 0, i, 0))
    return pl.pallas_call(
        functools.partial(_inproj_kernel, width=width),
        grid=(b, s // rows),
        in_specs=[
            pl.BlockSpec((1, rows, d), lambda i, t: (i, t, 0)),
            pl.BlockSpec((1, d), lambda i, t: (0, 0)),
            pl.BlockSpec(w_bf16.shape, lambda i, t: (0, 0)),
        ],
        out_specs=[nat_spec, nat_spec, nat_spec, nat_spec, u_spec, nat_spec],
        out_shape=[nat, nat, nat, nat, u_shape, nat],
        compiler_params=pltpu.CompilerParams(
            dimension_semantics=("arbitrary", "arbitrary"), vmem_limit_bytes=VMEM_LIMIT),
        name="inproj",
    )(x, gain.reshape(1, d), w_bf16)


def _zoh(lam_re, lam_im, dt):
    mag = jnp.exp(lam_re * dt)
    a_re = mag * jnp.cos(lam_im * dt)
    a_im = mag * jnp.sin(lam_im * dt)
    den = lam_re * lam_re + lam_im * lam_im
    num_re = a_re - 1.0
    coef_re = (num_re * lam_re + a_im * lam_im) / den
    coef_im = (a_im * lam_re - num_re * lam_im) / den
    return a_re, a_im, coef_re, coef_im


def _zoh_kernel(lre_ref, lim_ref, ldt_ref, lre_rep_ref, lim_rep_ref, bre_ref, bim_ref,
                are_ref, aim_ref, bbre_ref, bbim_ref):
    dt = jnp.exp(ldt_ref[...])
    a_re, a_im, _, _ = _zoh(lre_ref[...], lim_ref[...], dt)
    are_ref[...] = a_re
    aim_ref[...] = a_im
    _, _, coef_re, coef_im = _zoh(lre_rep_ref[...], lim_rep_ref[...], dt)
    b_re = bre_ref[...]
    b_im = bim_ref[...]
    bbre_ref[...] = coef_re * b_re - coef_im * b_im
    bbim_ref[...] = coef_re * b_im + coef_im * b_re


def _ssm_params(lam_re, lam_im, log_dt, b_re, b_im, c_re, c_im):
    g, n = lam_re.shape
    c = b_re.shape[-1]
    rep = lambda a: jnp.repeat(a, c, axis=1)
    small = jax.ShapeDtypeStruct((g, n), F32)
    wide = jax.ShapeDtypeStruct((g, n * c), F32)
    a_re, a_im, bb_re, bb_im = pl.pallas_call(
        _zoh_kernel, out_shape=[small, small, wide, wide], name="zoh",
    )(lam_re, lam_im, log_dt.reshape(g, 1), rep(lam_re), rep(lam_im),
      b_re.reshape(g, n * c), b_im.reshape(g, n * c))

    nch = g // GROUPS_PER_CHUNK
    eye = jnp.eye(GROUPS_PER_CHUNK, dtype=bool)

    def in_blockdiag(bb):
        t = bb.reshape(nch, GROUPS_PER_CHUNK, n, c).transpose(0, 1, 3, 2)
        m = jnp.where(eye[None, :, None, :, None], t[:, :, :, None, :], 0.0)
        return m.reshape(nch, GROUPS_PER_CHUNK * c, GROUPS_PER_CHUNK * n)

    def out_blockdiag(cc):
        t = cc.reshape(nch, GROUPS_PER_CHUNK, c, n).transpose(0, 1, 3, 2)
        m = jnp.where(eye[None, :, None, :, None], t[:, :, :, None, :], 0.0)
        return m.reshape(nch, GROUPS_PER_CHUNK * n, GROUPS_PER_CHUNK * c)

    w_b = jnp.concatenate([in_blockdiag(bb_re), in_blockdiag(bb_im)], axis=-1).astype(BF16)
    return (a_re.reshape(1, g * n), a_im.reshape(1, g * n), w_b,
            out_blockdiag(c_re).astype(BF16), out_blockdiag(c_im).astype(BF16))


def _attn_kernel(q_ref, k_ref, v_ref, ga_ref, gq_ref, gk_ref, o_ref,
                 qn, kn, x4, ops, res, bias_full, bias_first):
    seq = qn.shape[0]
    nblk = seq // QBLK
    lane = lax.broadcasted_iota(jnp.int32, (1, LANES), 1)
    head0 = lane < HEAD_DIM
    rr = lax.broadcasted_iota(jnp.int32, (LANES, LANES), 0) < HEAD_DIM
    cc = lax.broadcasted_iota(jnp.int32, (LANES, LANES), 1) < HEAD_DIM
    head_mean = jnp.where(rr == cc, 1.0 / HEAD_DIM, 0.0).astype(BF16)
    gq = gq_ref[...] * (HEAD_DIM ** -0.5 * LOG2E)
    gk = gk_ref[...]

    def blk(i, n=QBLK):
        start = i * QBLK
        return pl.ds(start if isinstance(i, int) else pl.multiple_of(start, QBLK), n)

    def write_operands(rows, qv, kv, vv):
        kb = kv.astype(BF16)
        vb = vv.astype(BF16)
        zero = jnp.zeros_like(kb)
        ops[0, rows, :] = qv.astype(BF16)
        ops[1, rows, :] = jnp.where(head0, kb, zero)
        ops[2, rows, :] = jnp.where(head0, zero, kb)
        ops[3, rows, :] = jnp.where(head0, vb, zero)
        ops[4, rows, :] = jnp.where(head0, zero, vb)

    def rms(x, gain):
        x2 = x * x
        hi = x2.astype(BF16)
        lo = (x2 - hi.astype(F32)).astype(BF16)
        ms = (jnp.dot(hi, head_mean, preferred_element_type=F32)
              + jnp.dot(lo, head_mean, preferred_element_type=F32))
        return x * lax.rsqrt(ms + EPS) * gain

    def norm_block(i, carry):
        rows = blk(i)
        qv = rms(q_ref[0, rows, :], gq)
        kv = rms(k_ref[0, rows, :], gk)
        qn[rows, :] = qv
        kn[rows, :] = kv
        write_operands(rows, qv, kv, v_ref[0, rows, :])
        return carry

    def to_mod4(c, carry):
        src = pl.ds((c >> 2) + (c & 3) * (4 * QBLK), QBLK, stride=4)
        rows = blk(c)
        qv, kv, vv = qn[src, :], kn[src, :], v_ref.at[0][src, :]
        x4[0, rows, :] = qv
        x4[1, rows, :] = kv
        x4[2, rows, :] = vv
        write_operands(rows, qv, kv, vv)
        return carry

    def to_mod16(c, carry):
        src = pl.ds((c >> 2) * (4 * QBLK) + (c & 3), QBLK, stride=4)
        write_operands(blk(c), x4.at[0][src, :], x4.at[1][src, :], x4.at[2][src, :])
        return carry

    qi = lax.broadcasted_iota(jnp.int32, (QBLK, 2 * QBLK), 0)
    kj = lax.broadcasted_iota(jnp.int32, (QBLK, 2 * QBLK), 1)
    bias_full[...] = jnp.where((kj >= qi) & (kj <= qi + QBLK), 0.0, NEG)
    qi1 = lax.broadcasted_iota(jnp.int32, (QBLK, QBLK), 0)
    kj1 = lax.broadcasted_iota(jnp.int32, (QBLK, QBLK), 1)
    bias_first[...] = jnp.where(kj1 <= qi1, 0.0, NEG)

    one0 = jnp.where(head0, 1.0, 0.0).astype(BF16)
    one1 = jnp.where(head0, 0.0, 1.0).astype(BF16)

    def unit(b, first, cfg, dst):
        nk = QBLK if first else 2 * QBLK
        qrows = blk(b)
        krows = qrows if first else blk(b - 1, nk)
        ones0 = jnp.broadcast_to(one0, (nk, LANES))
        ones1 = jnp.broadcast_to(one1, (nk, LANES))
        kcat = jnp.concatenate([ops[1, krows, :], ops[2, krows, :]], axis=0)
        s2 = lax.dot_general(ops[0, qrows, :], kcat, (((1,), (1,)), ((), ())),
                             preferred_element_type=F32)
        bias = bias_first[...] if first else bias_full[...]
        es, ms = [], []
        for h in range(2):
            s = s2[:, h * nk:(h + 1) * nk] + bias
            m = jnp.max(s, axis=-1, keepdims=True)
            es.append(jnp.exp2(s - m).astype(BF16))
            ms.append(m)
        rhs = jnp.concatenate([
            jnp.concatenate([ops[3, krows, :], ones0], axis=1),
            jnp.concatenate([ops[4, krows, :], ones1], axis=1)], axis=0)
        o = jnp.dot(jnp.concatenate(es, axis=1), rhs, preferred_element_type=F32)
        res.at[cfg, 0][dst, :] = jnp.where(head0, ms[0], ms[1])
        res.at[cfg, 1][dst, :] = o[:, :LANES]
        res.at[cfg, 2][dst, :] = o[:, LANES:]

    def all_units(cfg, dil):
        per_class = seq // (dil * QBLK)
        for b in range(nblk):
            if dil == 16:
                dst = pl.ds((b // 4) * (4 * QBLK) + b % 4, QBLK, stride=4)
            else:
                dst = blk(b)
            unit(b, b % per_class == 0, cfg, dst)

    assert DILATED_CONFIGS == ((128, 1), (512, 4), (2048, 16)) and nblk == 16
    lax.fori_loop(0, nblk, norm_block, 0, unroll=4)
    all_units(0, 1)
    lax.fori_loop(0, nblk, to_mod4, 0, unroll=2)
    all_units(1, 4)
    lax.fori_loop(0, nblk, to_mod16, 0, unroll=2)
    all_units(2, 16)

    def combine(parts):
        a, b = parts
        d = a[0] - b[0]
        t = jnp.exp2(-jnp.abs(d))
        wa = jnp.where(d >= 0, 1.0, t)
        wb = jnp.where(d >= 0, t, 1.0)
        return jnp.maximum(a[0], b[0]), wa * a[1] + wb * b[1], wa * a[2] + wb * b[2]

    def merge_mod4(c, carry):
        rows = blk(c)
        dst = pl.ds((c >> 2) + (c & 3) * (4 * QBLK), QBLK, stride=4)
        parts = [[res[cfg, k, rows, :] for k in range(3)] for cfg in (1, 2)]
        for k, val in enumerate(combine(parts)):
            x4.at[k][dst, :] = val
        return carry

    def merge_out(i, carry):
        rows = blk(i)
        parts = [[res[0, k, rows, :] for k in range(3)], [x4[k, rows, :] for k in range(3)]]
        _, num, den = combine(parts)
        g = ga_ref[0, rows, :]
        o_ref[0, rows, :] = (num / den * (g * _sigmoid(g))).astype(o_ref.dtype)
        return carry

    lax.fori_loop(0, nblk, merge_mod4, 0, unroll=2)
    lax.fori_loop(0, nblk, merge_out, 0, unroll=2)


def _attention(q, k, v, gate, q_gain, k_gain):
    b, s, width = q.shape
    pairs = width // LANES
    ncfg = len(DILATED_CONFIGS)
    spec = pl.BlockSpec((1, s, LANES), lambda i, h: (i, 0, h))
    gspec = pl.BlockSpec((1, LANES), lambda i, h: (0, 0))
    tile2 = lambda g: jnp.concatenate([g, g]).reshape(1, LANES)
    return pl.pallas_call(
        _attn_kernel,
        grid=(b, pairs),
        in_specs=[spec, spec, spec, spec, gspec, gspec],
        out_specs=spec,
        out_shape=jax.ShapeDtypeStruct((b, s, width), BF16),
        scratch_shapes=[
            pltpu.VMEM((s, LANES), F32), pltpu.VMEM((s, LANES), F32),
            pltpu.VMEM((3, s, LANES), F32),
            pltpu.VMEM((5, s, LANES), BF16),
            pltpu.VMEM((ncfg, 3, s, LANES), F32),
            pltpu.VMEM((QBLK, 2 * QBLK), F32), pltpu.VMEM((QBLK, QBLK), F32),
        ],
        compiler_params=pltpu.CompilerParams(
            dimension_semantics=("arbitrary", "arbitrary"), vmem_limit_bytes=VMEM_LIMIT),
        name="attn",
    )(q, k, v, gate, tile2(q_gain), tile2(k_gain))


def _ssm_kernel(u_ref, wb_ref, wcr_ref, wci_ref, are_ref, aim_ref, d_ref, y_ref,
                ut, xre, xim, sre, sim, *, batch):
    nslab, nj = u_ref.shape[0], u_ref.shape[1]
    steps = nslab * TIME_TILE

    @pl.when(pl.program_id(0) == 0)
    def _():
        sre[...] = jnp.zeros_like(sre)
        sim[...] = jnp.zeros_like(sim)

    for c in range(nslab):
        for j in range(nj):
            src = u_ref.at[c, j]
            for t in range(TIME_TILE):
                ut[pl.ds((c * TIME_TILE + t) * batch, batch), j * LANES:(j + 1) * LANES] = (
                    src[pl.ds(t, batch, stride=TIME_TILE), :])

    for j in range(nj):
        sl = slice(j * STATE_CHUNK, (j + 1) * STATE_CHUNK)
        bu = jnp.dot(ut[:, j * LANES:(j + 1) * LANES].astype(BF16), wb_ref[j],
                     preferred_element_type=F32)
        xre[:, sl] = bu[:, :STATE_CHUNK]
        xim[:, sl] = bu[:, STATE_CHUNK:]

    for j in range(nj):
        sl = slice(j * STATE_CHUNK, (j + 1) * STATE_CHUNK)
        a_re = jnp.broadcast_to(are_ref[:, sl], (batch, STATE_CHUNK))
        a_im = jnp.broadcast_to(aim_ref[:, sl], (batch, STATE_CHUNK))

        def step(t, carry, sl=sl, a_re=a_re, a_im=a_im):
            x_re, x_im = carry
            rows = pl.ds(pl.multiple_of(t * batch, batch), batch)
            n_re = a_re * x_re - a_im * x_im + xre[rows, sl]
            n_im = a_re * x_im + a_im * x_re + xim[rows, sl]
            xre[rows, sl] = n_re
            xim[rows, sl] = n_im
            return n_re, n_im

        f_re, f_im = lax.fori_loop(0, steps, step, (sre[:, sl], sim[:, sl]), unroll=8)
        sre[:, sl] = f_re
        sim[:, sl] = f_im

    sqrt_half = math.sqrt(0.5)
    for j in range(nj):
        sl = slice(j * STATE_CHUNK, (j + 1) * STATE_CHUNK)
        lanes = slice(j * LANES, (j + 1) * LANES)
        y = (jnp.dot(xre[:, sl].astype(BF16), wcr_ref[j], preferred_element_type=F32)
             - jnp.dot(xim[:, sl].astype(BF16), wci_ref[j], preferred_element_type=F32))
        y = y + d_ref[:, lanes] * ut[:, lanes]
        yg = 0.5 * y * (1.0 + lax.erf(y * sqrt_half))
        for c in range(nslab):
            dst = y_ref.at[c, j]
            for t in range(TIME_TILE):
                r0 = (c * TIME_TILE + t) * batch
                dst[pl.ds(t, batch, stride=TIME_TILE), :] = yg[r0:r0 + batch, :]


def _ssm(u_tiles, batch, w_b, w_cre, w_cim, a_re, a_im, d_skip):
    nslab_total, nj, _, _ = u_tiles.shape
    nslab = SSM_STEPS // TIME_TILE
    rows = SSM_STEPS * batch
    nstate = a_re.shape[1]
    tile_spec = pl.BlockSpec((nslab, nj, batch * TIME_TILE, LANES), lambda t: (t, 0, 0, 0))
    full = lambda a: pl.BlockSpec(a.shape, lambda t: (0,) * a.ndim)
    d2 = d_skip.reshape(1, -1)
    return pl.pallas_call(
        functools.partial(_ssm_kernel, batch=batch),
        grid=(nslab_total // nslab,),
        in_specs=[tile_spec, full(w_b), full(w_cre), full(w_cim), full(a_re), full(a_im), full(d2)],
        out_specs=tile_spec,
        out_shape=jax.ShapeDtypeStruct(u_tiles.shape, F32),
        scratch_shapes=[
            pltpu.VMEM((rows, nj * LANES), F32),
            pltpu.VMEM((rows, nstate), F32), pltpu.VMEM((rows, nstate), F32),
            pltpu.VMEM((batch, nstate), F32), pltpu.VMEM((batch, nstate), F32),
        ],
        compiler_params=pltpu.CompilerParams(
            dimension_semantics=("arbitrary",), vmem_limit_bytes=VMEM_LIMIT),
        name="ssm",
    )(u_tiles, w_b, w_cre, w_cim, a_re, a_im, d2)


def _out_kernel(attn_ref, y_ref, gs_ref, x_ref, p_ref, wglu_ref, bglu_ref, wout_ref, gple_ref,
                wg_ref, wp_ref, o_ref):
    nslab, nj = y_ref.shape[0], y_ref.shape[1]
    rows = nslab * TIME_TILE
    aw = attn_ref.shape[-1]
    yg = jnp.concatenate([y_ref[:, j].reshape(rows, LANES) for j in range(nj)], axis=1)
    glu = jnp.dot(yg.astype(BF16), wglu_ref[...], preferred_element_type=F32) + bglu_ref[...]
    gs = gs_ref[0]
    ssm = yg * _sigmoid(glu) * (gs * _sigmoid(gs))
    h = (x_ref[0]
         + jnp.dot(attn_ref[0], wout_ref[:aw, :], preferred_element_type=F32)
         + jnp.dot(ssm.astype(BF16), wout_ref[aw:, :], preferred_element_type=F32))
    ms = jnp.mean(h * h, axis=-1, keepdims=True)
    hn = (h * lax.rsqrt(ms + EPS) * gple_ref[...]).astype(BF16)
    gate = _sigmoid(jnp.dot(hn, wg_ref[...], preferred_element_type=F32))
    pp = jnp.dot(p_ref[0].astype(BF16), wp_ref[...], preferred_element_type=F32)
    o_ref[0] = h + gate * pp


def _out_stage(attn, y_tiles, gate_s, x, p, w_glu, b_glu, w_out, ple_gain, w_gate, w_proj):
    b, s, d = x.shape
    rows = ROW_TILE
    nj = y_tiles.shape[1]
    nat = lambda a: pl.BlockSpec((1, rows, a.shape[-1]), lambda i, t: (i, t, 0))
    full = lambda a: pl.BlockSpec(a.shape, lambda i, t: (0,) * a.ndim)
    y_spec = pl.BlockSpec((rows // TIME_TILE, nj, TIME_TILE, LANES), lambda i, t: (t, 0, i, 0))
    b2 = b_glu.reshape(1, -1)
    g2 = ple_gain.reshape(1, -1)
    return pl.pallas_call(
        _out_kernel,
        grid=(b, s // rows),
        in_specs=[nat(attn), y_spec, nat(gate_s), nat(x), nat(p), full(w_glu), full(b2), full(w_out),
                  full(g2), full(w_gate), full(w_proj)],
        out_specs=nat(x),
        out_shape=jax.ShapeDtypeStruct(x.shape, x.dtype),
        compiler_params=pltpu.CompilerParams(
            dimension_semantics=("arbitrary", "arbitrary"), vmem_limit_bytes=VMEM_LIMIT),
        name="outstage",
    )(attn, y_tiles, gate_s, x, p, w_glu, b2, w_out, g2, w_gate, w_proj)


def kernel(x, p, mix_norm, w_in, q_norm, k_norm, lambda_re, lambda_im, log_dt, b_re, b_im, c_re, c_im,
           d_skip, w_glu, b_glu, w_out, ple_norm, w_ple_gate, w_ple_proj):
    batch = x.shape[0]
    h = x
    for i in range(w_in.shape[0]):
        q, k, v, gate_a, u_tiles, gate_s = _inproj(h, mix_norm[i], w_in[i].astype(BF16))
        a_re, a_im, w_b, w_cre, w_cim = _ssm_params(
            lambda_re[i], lambda_im[i], log_dt[i], b_re[i], b_im[i], c_re[i], c_im[i])
        attn = _attention(q, k, v, gate_a, q_norm[i], k_norm[i])
        y_tiles = _ssm(u_tiles, batch, w_b, w_cre, w_cim, a_re, a_im, d_skip[i])
        h = _out_stage(attn, y_tiles, gate_s, h, p[i], w_glu[i].astype(BF16), b_glu[i],
                       w_out[i].astype(BF16), ple_norm[i], w_ple_gate[i].astype(BF16),
                       w_ple_proj[i].astype(BF16))
    return h
```

```python
import functools
import math

import jax
import jax.numpy as jnp
from jax import lax
from jax.experimental import pallas as pl
from jax.experimental.pallas import tpu as pltpu

F32 = jnp.float32
BF16 = jnp.bfloat16

HEAD_DIM = 64
SSM_GROUP = 16
SSM_STATE = 64
DILATED_CONFIGS = ((128, 1), (512, 4), (2048, 16))
EPS = 1e-6

LANES = 128
QBLK = 128
NEG = -1e30
LOG2E = math.log2(math.e)
ROW_TILE = 512
SSM_STEPS = 32
TIME_TILE = 8
GROUPS_PER_CHUNK = LANES // SSM_GROUP
STATE_CHUNK = GROUPS_PER_CHUNK * SSM_STATE
VMEM_LIMIT = 48 * 1024 * 1024


def _sigmoid(x):
    return jax.nn.sigmoid(x)


def _inproj_kernel(x_ref, g_ref, w_ref, q_ref, k_ref, v_ref, ga_ref, u_ref, gs_ref, *, width):
    x = x_ref[0]
    ms = jnp.mean(x * x, axis=-1, keepdims=True)
    xn = (x * lax.rsqrt(ms + EPS) * g_ref[...]).astype(BF16)

    def proj(idx):
        return jnp.dot(xn, w_ref[:, idx * width:(idx + 1) * width], preferred_element_type=F32)

    q_ref[0] = proj(0)
    k_ref[0] = proj(1)
    v_ref[0] = proj(2)
    ga_ref[0] = proj(3)
    u = proj(4)
    for c in range(u.shape[0] // TIME_TILE):
        for j in range(width // LANES):
            u_ref[c, j] = u[c * TIME_TILE:(c + 1) * TIME_TILE, j * LANES:(j + 1) * LANES]
    gs_ref[0] = proj(5)


def _inproj(x, gain, w_bf16):
    b, s, d = x.shape
    width = w_bf16.shape[1] // 6
    nj = width // LANES
    rows = ROW_TILE
    nat = jax.ShapeDtypeStruct((b, s, width), F32)
    nat_spec = pl.BlockSpec((1, rows, width), lambda i, t: (i, t, 0))
    u_shape = jax.ShapeDtypeStruct((s // TIME_TILE, nj, b * TIME_TILE, LANES), F32)
    u_spec = pl.BlockSpec((rows // TIME_TILE, nj, TIME_TILE, LANES), lambda i, t: (t, 0, i, 0))
    return pl.pallas_call(
        functools.partial(_inproj_kernel, width=width),
        grid=(b, s // rows),
        in_specs=[
            pl.BlockSpec((1, rows, d), lambda i, t: (i, t, 0)),
            pl.BlockSpec((1, d), lambda i, t: (0, 0)),
            pl.BlockSpec(w_bf16.shape, lambda i, t: (0, 0)),
        ],
        out_specs=[nat_spec, nat_spec, nat_spec, nat_spec, u_spec, nat_spec],
        out_shape=[nat, nat, nat, nat, u_shape, nat],
        compiler_params=pltpu.CompilerParams(
            dimension_semantics=("arbitrary", "arbitrary"), vmem_limit_bytes=VMEM_LIMIT),
        name="inproj",
    )(x, gain.reshape(1, d), w_bf16)


def _zoh(lam_re, lam_im, dt):
    mag = jnp.exp(lam_re * dt)
    a_re = mag * jnp.cos(lam_im * dt)
    a_im = mag * jnp.sin(lam_im * dt)
    den = lam_re * lam_re + lam_im * lam_im
    num_re = a_re - 1.0
    coef_re = (num_re * lam_re + a_im * lam_im) / den
    coef_im = (a_im * lam_re - num_re * lam_im) / den
    return a_re, a_im, coef_re, coef_im


def _zoh_kernel(lre_ref, lim_ref, ldt_ref, lre_rep_ref, lim_rep_ref, bre_ref, bim_ref,
                are_ref, aim_ref, bbre_ref, bbim_ref):
    dt = jnp.exp(ldt_ref[...])
    a_re, a_im, _, _ = _zoh(lre_ref[...], lim_ref[...], dt)
    are_ref[...] = a_re
    aim_ref[...] = a_im
    _, _, coef_re, coef_im = _zoh(lre_rep_ref[...], lim_rep_ref[...], dt)
    b_re = bre_ref[...]
    b_im = bim_ref[...]
    bbre_ref[...] = coef_re * b_re - coef_im * b_im
    bbim_ref[...] = coef_re * b_im + coef_im * b_re


def _ssm_params(lam_re, lam_im, log_dt, b_re, b_im, c_re, c_im):
    g, n = lam_re.shape
    c = b_re.shape[-1]
    rep = lambda a: jnp.repeat(a, c, axis=1)
    small = jax.ShapeDtypeStruct((g, n), F32)
    wide = jax.ShapeDtypeStruct((g, n * c), F32)
    a_re, a_im, bb_re, bb_im = pl.pallas_call(
        _zoh_kernel, out_shape=[small, small, wide, wide], name="zoh",
    )(lam_re, lam_im, log_dt.reshape(g, 1), rep(lam_re), rep(lam_im),
      b_re.reshape(g, n * c), b_im.reshape(g, n * c))

    nch = g // GROUPS_PER_CHUNK
    eye = jnp.eye(GROUPS_PER_CHUNK, dtype=bool)

    def in_blockdiag(bb):
        t = bb.reshape(nch, GROUPS_PER_CHUNK, n, c).transpose(0, 1, 3, 2)
        m = jnp.where(eye[None, :, None, :, None], t[:, :, :, None, :], 0.0)
        return m.reshape(nch, GROUPS_PER_CHUNK * c, GROUPS_PER_CHUNK * n)

    def out_blockdiag(cc):
        t = cc.reshape(nch, GROUPS_PER_CHUNK, c, n).transpose(0, 1, 3, 2)
        m = jnp.where(eye[None, :, None, :, None], t[:, :, :, None, :], 0.0)
        return m.reshape(nch, GROUPS_PER_CHUNK * n, GROUPS_PER_CHUNK * c)

    w_b = jnp.concatenate([in_blockdiag(bb_re), in_blockdiag(bb_im)], axis=-1).astype(BF16)
    return (a_re.reshape(1, g * n), a_im.reshape(1, g * n), w_b,
            out_blockdiag(c_re).astype(BF16), out_blockdiag(c_im).astype(BF16))


def _attn_kernel(q_ref, k_ref, v_ref, ga_ref, gq_ref, gk_ref, o_ref,
                 qn, kn, ops_a, kt_a, ops_b, kt_b, res, out32, bias_full, bias_first, bias_mod4):
    seq = qn.shape[0]
    nblk = seq // QBLK
    per_class = nblk // 4
    assert DILATED_CONFIGS == ((128, 1), (512, 4), (2048, 16)) and nblk == 16
    lane = lax.broadcasted_iota(jnp.int32, (1, LANES), 1)
    head0 = lane < HEAD_DIM
    rr = lax.broadcasted_iota(jnp.int32, (2 * LANES, 2 * LANES), 0) // HEAD_DIM
    cc = lax.broadcasted_iota(jnp.int32, (2 * LANES, 2 * LANES), 1) // HEAD_DIM
    head_mean = jnp.where(rr == cc, 1.0 / HEAD_DIM, 0.0).astype(BF16)
    head_mean = jnp.concatenate([head_mean[:LANES], head_mean[:LANES], head_mean[LANES:], head_mean[LANES:]], axis=0)
    dim0 = lax.broadcasted_iota(jnp.int32, (LANES, 1), 0) < HEAD_DIM
    gq = gq_ref[...] * (HEAD_DIM ** -0.5 * LOG2E)
    gk = gk_ref[...]
    one0 = jnp.where(head0, 1.0, 0.0).astype(BF16)
    one1 = jnp.where(head0, 0.0, 1.0).astype(BF16)

    def blk(i, n=QBLK):
        return pl.ds(i * QBLK, n)

    def write_operands(ops, kt, b, qv, kv, vv):
        rows = blk(b)
        vb = vv.astype(BF16)
        zero = jnp.zeros_like(vb)
        ops[0, rows, :] = qv.astype(BF16)
        ops[1, rows, :] = jnp.where(head0, vb, zero)
        ops[2, rows, :] = jnp.where(head0, zero, vb)
        ktb = kv.T.astype(BF16)
        kt[0, :, b * QBLK:(b + 1) * QBLK] = jnp.where(dim0, ktb, zero)
        kt[1, :, b * QBLK:(b + 1) * QBLK] = jnp.where(dim0, zero, ktb)

    def split(x):
        hi = x.astype(BF16)
        return hi, (x - hi.astype(F32)).astype(BF16)

    def norm_block(b):
        rows = blk(b)
        q = q_ref[0, rows, :]
        k = k_ref[0, rows, :]
        ms = jnp.dot(jnp.concatenate(split(q * q) + split(k * k), axis=1), head_mean,
                     preferred_element_type=F32)
        qv = q * lax.rsqrt(ms[:, :LANES] + EPS) * gq
        kv = k * lax.rsqrt(ms[:, LANES:] + EPS) * gk
        qn[rows, :] = qv
        kn[rows, :] = kv
        write_operands(ops_a, kt_a, b, qv, kv, v_ref[0, rows, :])

    def to_mod4(c):
        src = pl.ds((c // per_class) + (c % per_class) * (4 * QBLK), QBLK, stride=4)
        write_operands(ops_b, kt_b, c, qn[src, :], kn[src, :], v_ref.at[0][src, :])

    qi = lax.broadcasted_iota(jnp.int32, (QBLK, 2 * QBLK), 0)
    kj = lax.broadcasted_iota(jnp.int32, (QBLK, 2 * QBLK), 1)
    bias_full[...] = jnp.where((kj >= qi) & (kj <= qi + QBLK), 0.0, NEG)
    qi1 = lax.broadcasted_iota(jnp.int32, (QBLK, QBLK), 0)
    kj1 = lax.broadcasted_iota(jnp.int32, (QBLK, QBLK), 1)
    bias_first[...] = jnp.where(kj1 <= qi1, 0.0, NEG)
    same4 = ((qi1 - kj1) & 3) == 0
    far = jnp.where(same4, 0.0, NEG)
    prev = jnp.where(same4, jnp.where(kj1 >= qi1, 1.0, 0.0), jnp.where(kj1 >= qi1, 0.0, NEG))
    cur = jnp.where(kj1 <= qi1, jnp.where(same4, 1.0, 0.0), NEG)
    for t, tile in enumerate((far, far, prev, cur)):
        bias_mod4[:, t * QBLK:(t + 1) * QBLK] = tile

    def unit(ops, kt, b, k0, nk, bias):
        krows = pl.ds(k0, nk)
        kcat = jnp.concatenate([kt[0, :, k0:k0 + nk], kt[1, :, k0:k0 + nk]], axis=1)
        s2 = jnp.dot(ops[0, blk(b), :], kcat, preferred_element_type=F32)
        es, ms = [], []
        for h in range(2):
            s = s2[:, h * nk:(h + 1) * nk] + bias
            m = jnp.max(s, axis=-1, keepdims=True)
            es.append(jnp.exp2(s - m).astype(BF16))
            ms.append(m)
        rhs = jnp.concatenate([
            jnp.concatenate([ops[1, krows, :], jnp.broadcast_to(one0, (nk, LANES))], axis=1),
            jnp.concatenate([ops[2, krows, :], jnp.broadcast_to(one1, (nk, LANES))], axis=1)], axis=0)
        o = jnp.dot(jnp.concatenate(es, axis=1), rhs, preferred_element_type=F32)
        return jnp.where(head0, ms[0], ms[1]), o[:, :LANES], o[:, LANES:]

    def unit_natural(b):
        if b == 0:
            part = unit(ops_a, kt_a, b, 0, QBLK, bias_first[...])
        else:
            part = unit(ops_a, kt_a, b, (b - 1) * QBLK, 2 * QBLK, bias_full[...])
        for k in range(3):
            res[k, blk(b), :] = part[k]

    def unit_mod4(b):
        r, i = b // per_class, b % per_class
        rows = pl.ds(r + i * (4 * QBLK), QBLK, stride=4)
        seg = r * per_class * QBLK
        near = min(i + 1, 2)
        parts = [[res.at[k][rows, :] for k in range(3)],
                 unit(ops_b, kt_b, b, seg + (i + 1 - near) * QBLK, near * QBLK,
                      bias_mod4[:, (per_class - near) * QBLK:])]
        if i >= 2:
            parts.append(unit(ops_b, kt_b, b, seg, (i - 1) * QBLK, bias_mod4[:, :(i - 1) * QBLK]))
        mx = functools.reduce(jnp.maximum, [p[0] for p in parts])
        ws = [jnp.exp2(p[0] - mx) for p in parts]
        num = functools.reduce(jnp.add, [w * p[1] for w, p in zip(ws, parts)])
        den = functools.reduce(jnp.add, [w * p[2] for w, p in zip(ws, parts)])
        g = ga_ref.at[0][rows, :]
        out32[rows, :] = num / den * (g * _sigmoid(g))

    def cast_out(i):
        o_ref[0, blk(i), :] = out32[blk(i), :].astype(o_ref.dtype)

    for b in range(nblk):
        norm_block(b)
    for b in range(nblk):
        unit_natural(b)
        to_mod4(b)
    for b in range(nblk):
        unit_mod4(b)
    for i in range(nblk):
        cast_out(i)


def _attention(q, k, v, gate, q_gain, k_gain):
    b, s, width = q.shape
    pairs = width // LANES
    spec = pl.BlockSpec((1, s, LANES), lambda i, h: (i, 0, h))
    gspec = pl.BlockSpec((1, LANES), lambda i, h: (0, 0))
    tile2 = lambda g: jnp.concatenate([g, g]).reshape(1, LANES)
    return pl.pallas_call(
        _attn_kernel,
        grid=(b, pairs),
        in_specs=[spec, spec, spec, spec, gspec, gspec],
        out_specs=spec,
        out_shape=jax.ShapeDtypeStruct((b, s, width), BF16),
        scratch_shapes=[
            pltpu.VMEM((s, LANES), F32), pltpu.VMEM((s, LANES), F32),
            pltpu.VMEM((3, s, LANES), BF16), pltpu.VMEM((2, LANES, s), BF16),
            pltpu.VMEM((3, s, LANES), BF16), pltpu.VMEM((2, LANES, s), BF16),
            pltpu.VMEM((3, s, LANES), F32),
            pltpu.VMEM((s, LANES), F32),
            pltpu.VMEM((QBLK, 2 * QBLK), F32), pltpu.VMEM((QBLK, QBLK), F32), pltpu.VMEM((QBLK, 4 * QBLK), F32),
        ],
        compiler_params=pltpu.CompilerParams(
            dimension_semantics=("arbitrary", "arbitrary"), vmem_limit_bytes=VMEM_LIMIT),
        name="attn",
    )(q, k, v, gate, tile2(q_gain), tile2(k_gain))


def _ssm_kernel(u_ref, wb_ref, wcr_ref, wci_ref, are_ref, aim_ref, d_ref, y_ref,
                ut, sre, sim, *, batch):
    nslab, nj = u_ref.shape[0], u_ref.shape[1]
    steps = nslab * TIME_TILE

    @pl.when(pl.program_id(0) == 0)
    def _():
        sre[...] = jnp.zeros_like(sre)
        sim[...] = jnp.zeros_like(sim)

    for c in range(nslab):
        for j in range(nj):
            src = u_ref.at[c, j]
            for t in range(TIME_TILE):
                ut[pl.ds((c * TIME_TILE + t) * batch, batch), j * LANES:(j + 1) * LANES] = (
                    src[pl.ds(t, batch, stride=TIME_TILE), :])

    sqrt_half = math.sqrt(0.5)
    for j in range(nj):
        sl = slice(j * STATE_CHUNK, (j + 1) * STATE_CHUNK)
        lanes = slice(j * LANES, (j + 1) * LANES)
        uj = ut[:, lanes]
        bu = jnp.dot(uj.astype(BF16), wb_ref[j], preferred_element_type=F32)
        a_re = jnp.broadcast_to(are_ref[:, sl], (batch, STATE_CHUNK))
        a_im = jnp.broadcast_to(aim_ref[:, sl], (batch, STATE_CHUNK))
        x_re, x_im = sre[:, sl], sim[:, sl]
        hist_re, hist_im = [], []
        for t in range(steps):
            b_re = bu[t * batch:(t + 1) * batch, :STATE_CHUNK]
            b_im = bu[t * batch:(t + 1) * batch, STATE_CHUNK:]
            x_re, x_im = a_re * x_re - a_im * x_im + b_re, a_re * x_im + a_im * x_re + b_im
            hist_re.append(x_re.astype(BF16))
            hist_im.append(x_im.astype(BF16))
        sre[:, sl] = x_re
        sim[:, sl] = x_im
        y = (jnp.dot(jnp.concatenate(hist_re, axis=0), wcr_ref[j], preferred_element_type=F32)
             - jnp.dot(jnp.concatenate(hist_im, axis=0), wci_ref[j], preferred_element_type=F32))
        y = y + d_ref[:, lanes] * uj
        yg = 0.5 * y * (1.0 + lax.erf(y * sqrt_half))
        for c in range(nslab):
            dst = y_ref.at[c, j]
            for t in range(TIME_TILE):
                r0 = (c * TIME_TILE + t) * batch
                dst[pl.ds(t, batch, stride=TIME_TILE), :] = yg[r0:r0 + batch, :]


def _ssm(u_tiles, batch, w_b, w_cre, w_cim, a_re, a_im, d_skip):
    nslab_total, nj, _, _ = u_tiles.shape
    nslab = SSM_STEPS // TIME_TILE
    rows = SSM_STEPS * batch
    nstate = a_re.shape[1]
    tile_spec = pl.BlockSpec((nslab, nj, batch * TIME_TILE, LANES), lambda t: (t, 0, 0, 0))
    full = lambda a: pl.BlockSpec(a.shape, lambda t: (0,) * a.ndim)
    d2 = d_skip.reshape(1, -1)
    return pl.pallas_call(
        functools.partial(_ssm_kernel, batch=batch),
        grid=(nslab_total // nslab,),
        in_specs=[tile_spec, full(w_b), full(w_cre), full(w_cim), full(a_re), full(a_im), full(d2)],
        out_specs=tile_spec,
        out_shape=jax.ShapeDtypeStruct(u_tiles.shape, F32),
        scratch_shapes=[
            pltpu.VMEM((rows, nj * LANES), F32),
            pltpu.VMEM((batch, nstate), F32), pltpu.VMEM((batch, nstate), F32),
        ],
        compiler_params=pltpu.CompilerParams(
            dimension_semantics=("arbitrary",), vmem_limit_bytes=VMEM_LIMIT),
        name="ssm",
    )(u_tiles, w_b, w_cre, w_cim, a_re, a_im, d2)


def _out_kernel(attn_ref, y_ref, gs_ref, x_ref, p_ref, wglu_ref, bglu_ref, wout_ref, gple_ref,
                wg_ref, wp_ref, o_ref):
    nslab, nj = y_ref.shape[0], y_ref.shape[1]
    rows = nslab * TIME_TILE
    aw = attn_ref.shape[-1]
    yg = jnp.concatenate([y_ref[:, j].reshape(rows, LANES) for j in range(nj)], axis=1)
    glu = jnp.dot(yg.astype(BF16), wglu_ref[...], preferred_element_type=F32) + bglu_ref[...]
    gs = gs_ref[0]
    ssm = yg * _sigmoid(glu) * (gs * _sigmoid(gs))
    h = (x_ref[0]
         + jnp.dot(attn_ref[0], wout_ref[:aw, :], preferred_element_type=F32)
         + jnp.dot(ssm.astype(BF16), wout_ref[aw:, :], preferred_element_type=F32))
    ms = jnp.mean(h * h, axis=-1, keepdims=True)
    hn = (h * lax.rsqrt(ms + EPS) * gple_ref[...]).astype(BF16)
    gate = _sigmoid(jnp.dot(hn, wg_ref[...], preferred_element_type=F32))
    pp = jnp.dot(p_ref[0].astype(BF16), wp_ref[...], preferred_element_type=F32)
    o_ref[0] = h + gate * pp


def _out_stage(attn, y_tiles, gate_s, x, p, w_glu, b_glu, w_out, ple_gain, w_gate, w_proj):
    b, s, d = x.shape
    rows = ROW_TILE
    nj = y_tiles.shape[1]
    nat = lambda a: pl.BlockSpec((1, rows, a.shape[-1]), lambda i, t: (i, t, 0))
    full = lambda a: pl.BlockSpec(a.shape, lambda i, t: (0,) * a.ndim)
    y_spec = pl.BlockSpec((rows // TIME_TILE, nj, TIME_TILE, LANES), lambda i, t: (t, 0, i, 0))
    b2 = b_glu.reshape(1, -1)
    g2 = ple_gain.reshape(1, -1)
    return pl.pallas_call(
        _out_kernel,
        grid=(b, s // rows),
        in_specs=[nat(attn), y_spec, nat(gate_s), nat(x), nat(p), full(w_glu), full(b2), full(w_out),
                  full(g2), full(w_gate), full(w_proj)],
        out_specs=nat(x),
        out_shape=jax.ShapeDtypeStruct(x.shape, x.dtype),
        compiler_params=pltpu.CompilerParams(
            dimension_semantics=("arbitrary", "arbitrary"), vmem_limit_bytes=VMEM_LIMIT),
        name="outstage",
    )(attn, y_tiles, gate_s, x, p, w_glu, b2, w_out, g2, w_gate, w_proj)


def kernel(x, p, mix_norm, w_in, q_norm, k_norm, lambda_re, lambda_im, log_dt, b_re, b_im, c_re, c_im,
           d_skip, w_glu, b_glu, w_out, ple_norm, w_ple_gate, w_ple_proj):
    batch = x.shape[0]
    h = x
    for i in range(w_in.shape[0]):
        q, k, v, gate_a, u_tiles, gate_s = _inproj(h, mix_norm[i], w_in[i].astype(BF16))
        a_re, a_im, w_b, w_cre, w_cim = _ssm_params(
            lambda_re[i], lambda_im[i], log_dt[i], b_re[i], b_im[i], c_re[i], c_im[i])
        attn = _attention(q, k, v, gate_a, q_norm[i], k_norm[i])
        y_tiles = _ssm(u_tiles, batch, w_b, w_cre, w_cim, a_re, a_im, d_skip[i])
        h = _out_stage(attn, y_tiles, gate_s, h, p[i], w_glu[i].astype(BF16), b_glu[i],
                       w_out[i].astype(BF16), ple_norm[i], w_ple_gate[i].astype(BF16),
                       w_ple_proj[i].astype(BF16))
    return h
```

```python
import functools
import math

import jax
import jax.numpy as jnp
from jax import lax
from jax.experimental import pallas as pl
from jax.experimental.pallas import tpu as pltpu

F32 = jnp.float32
BF16 = jnp.bfloat16

HEAD_DIM = 64
SSM_GROUP = 16
SSM_STATE = 64
DILATED_CONFIGS = ((128, 1), (512, 4), (2048, 16))
EPS = 1e-6

LANES = 128
QBLK = 128
NEG = -1e30
LOG2E = math.log2(math.e)
ROW_TILE = 512
SSM_STEPS = 32
TIME_TILE = 8
GROUPS_PER_CHUNK = LANES // SSM_GROUP
STATE_CHUNK = GROUPS_PER_CHUNK * SSM_STATE
VMEM_LIMIT = 48 * 1024 * 1024


def _sigmoid(x):
    return jax.nn.sigmoid(x)


def _inproj_kernel(x_ref, g_ref, w_ref, q_ref, k_ref, v_ref, ga_ref, u_ref, gs_ref, *, width):
    x = x_ref[0]
    ms = jnp.mean(x * x, axis=-1, keepdims=True)
    xn = (x * lax.rsqrt(ms + EPS) * g_ref[...]).astype(BF16)

    def proj(idx):
        return jnp.dot(xn, w_ref[:, idx * width:(idx + 1) * width], preferred_element_type=F32)

    q_ref[0] = proj(0)
    k_ref[0] = proj(1)
    v_ref[0] = proj(2)
    ga_ref[0] = proj(3)
    u = proj(4)
    for c in range(u.shape[0] // TIME_TILE):
        for j in range(width // LANES):
            u_ref[c, j] = u[c * TIME_TILE:(c + 1) * TIME_TILE, j * LANES:(j + 1) * LANES]
    gs_ref[0] = proj(5)


def _inproj(x, gain, w_bf16):
    b, s, d = x.shape
    width = w_bf16.shape[1] // 6
    nj = width // LANES
    rows = ROW_TILE
    nat = jax.ShapeDtypeStruct((b, s, width), F32)
    nat_spec = pl.BlockSpec((1, rows, width), lambda i, t: (i, t, 0))
    u_shape = jax.ShapeDtypeStruct((s // TIME_TILE, nj, b * TIME_TILE, LANES), F32)
    u_spec = pl.BlockSpec((rows // TIME_TILE, nj, TIME_TILE, LANES), lambda i, t: (t, 0, i, 0))
    return pl.pallas_call(
        functools.partial(_inproj_kernel, width=width),
        grid=(b, s // rows),
        in_specs=[
            pl.BlockSpec((1, rows, d), lambda i, t: (i, t, 0)),
            pl.BlockSpec((1, d), lambda i, t: (0, 0)),
            pl.BlockSpec(w_bf16.shape, lambda i, t: (0, 0)),
        ],
        out_specs=[nat_spec, nat_spec, nat_spec, nat_spec, u_spec, nat_spec],
        out_shape=[nat, nat, nat, nat, u_shape, nat],
        compiler_params=pltpu.CompilerParams(
            dimension_semantics=("arbitrary", "arbitrary"), vmem_limit_bytes=VMEM_LIMIT),
        name="inproj",
    )(x, gain.reshape(1, d), w_bf16)


def _zoh(lam_re, lam_im, dt):
    mag = jnp.exp(lam_re * dt)
    a_re = mag * jnp.cos(lam_im * dt)
    a_im = mag * jnp.sin(lam_im * dt)
    den = lam_re * lam_re + lam_im * lam_im
    num_re = a_re - 1.0
    coef_re = (num_re * lam_re + a_im * lam_im) / den
    coef_im = (a_im * lam_re - num_re * lam_im) / den
    return a_re, a_im, coef_re, coef_im


def _zoh_kernel(lre_ref, lim_ref, ldt_ref, lre_rep_ref, lim_rep_ref, bre_ref, bim_ref,
                are_ref, aim_ref, bbre_ref, bbim_ref):
    dt = jnp.exp(ldt_ref[...])
    a_re, a_im, _, _ = _zoh(lre_ref[...], lim_ref[...], dt)
    are_ref[...] = a_re
    aim_ref[...] = a_im
    _, _, coef_re, coef_im = _zoh(lre_rep_ref[...], lim_rep_ref[...], dt)
    b_re = bre_ref[...]
    b_im = bim_ref[...]
    bbre_ref[...] = coef_re * b_re - coef_im * b_im
    bbim_ref[...] = coef_re * b_im + coef_im * b_re


def _ssm_params(lam_re, lam_im, log_dt, b_re, b_im, c_re, c_im):
    g, n = lam_re.shape
    c = b_re.shape[-1]
    rep = lambda a: jnp.repeat(a, c, axis=1)
    small = jax.ShapeDtypeStruct((g, n), F32)
    wide = jax.ShapeDtypeStruct((g, n * c), F32)
    a_re, a_im, bb_re, bb_im = pl.pallas_call(
        _zoh_kernel, out_shape=[small, small, wide, wide], name="zoh",
    )(lam_re, lam_im, log_dt.reshape(g, 1), rep(lam_re), rep(lam_im),
      b_re.reshape(g, n * c), b_im.reshape(g, n * c))

    nch = g // GROUPS_PER_CHUNK
    eye = jnp.eye(GROUPS_PER_CHUNK, dtype=bool)

    def in_blockdiag(bb):
        t = bb.reshape(nch, GROUPS_PER_CHUNK, n, c).transpose(0, 1, 3, 2)
        m = jnp.where(eye[None, :, None, :, None], t[:, :, :, None, :], 0.0)
        return m.reshape(nch, GROUPS_PER_CHUNK * c, GROUPS_PER_CHUNK * n)

    def out_blockdiag(cc):
        t = cc.reshape(nch, GROUPS_PER_CHUNK, c, n).transpose(0, 1, 3, 2)
        m = jnp.where(eye[None, :, None, :, None], t[:, :, :, None, :], 0.0)
        return m.reshape(nch, GROUPS_PER_CHUNK * n, GROUPS_PER_CHUNK * c)

    w_b = jnp.concatenate([in_blockdiag(bb_re), in_blockdiag(bb_im)], axis=-1).astype(BF16)
    return (a_re.reshape(1, g * n), a_im.reshape(1, g * n), w_b,
            out_blockdiag(c_re).astype(BF16), out_blockdiag(c_im).astype(BF16))


def _attn_kernel(q_ref, k_ref, v_ref, ga_ref, gq_ref, gk_ref, o_ref,
                 qn, kn, ops_a, kt_a, ops_b, kt_b, res, out32, bias_full, bias_first, bias_mod4):
    seq = qn.shape[0]
    nblk = seq // QBLK
    per_class = nblk // 4
    assert DILATED_CONFIGS == ((128, 1), (512, 4), (2048, 16)) and nblk == 16
    lane = lax.broadcasted_iota(jnp.int32, (1, LANES), 1)
    head0 = lane < HEAD_DIM
    rr = lax.broadcasted_iota(jnp.int32, (2 * LANES, 2 * LANES), 0) // HEAD_DIM
    cc = lax.broadcasted_iota(jnp.int32, (2 * LANES, 2 * LANES), 1) // HEAD_DIM
    head_mean = jnp.where(rr == cc, 1.0 / HEAD_DIM, 0.0).astype(BF16)
    head_mean = jnp.concatenate([head_mean[:LANES], head_mean[:LANES], head_mean[LANES:], head_mean[LANES:]], axis=0)
    dim0 = lax.broadcasted_iota(jnp.int32, (LANES, 1), 0) < HEAD_DIM
    gq = gq_ref[...] * (HEAD_DIM ** -0.5 * LOG2E)
    gk = gk_ref[...]
    one0 = jnp.where(head0, 1.0, 0.0).astype(BF16)
    one1 = jnp.where(head0, 0.0, 1.0).astype(BF16)

    def blk(i, n=QBLK):
        return pl.ds(i * QBLK, n)

    def write_operands(ops, kt, b, qv, kv, vv):
        rows = blk(b)
        vb = vv.astype(BF16)
        zero = jnp.zeros_like(vb)
        ops[0, rows, :] = qv.astype(BF16)
        ops[1, rows, :] = jnp.where(head0, vb, zero)
        ops[2, rows, :] = jnp.where(head0, zero, vb)
        ktb = kv.T.astype(BF16)
        kt[0, :, b * QBLK:(b + 1) * QBLK] = jnp.where(dim0, ktb, zero)
        kt[1, :, b * QBLK:(b + 1) * QBLK] = jnp.where(dim0, zero, ktb)

    def split(x):
        hi = x.astype(BF16)
        return hi, (x - hi.astype(F32)).astype(BF16)

    def norm_block(b):
        rows = blk(b)
        q = q_ref[0, rows, :]
        k = k_ref[0, rows, :]
        ms = jnp.dot(jnp.concatenate(split(q * q) + split(k * k), axis=1), head_mean,
                     preferred_element_type=F32)
        qv = q * lax.rsqrt(ms[:, :LANES] + EPS) * gq
        kv = k * lax.rsqrt(ms[:, LANES:] + EPS) * gk
        qn[rows, :] = qv
        kn[rows, :] = kv
        write_operands(ops_a, kt_a, b, qv, kv, v_ref[0, rows, :])

    def to_mod4(c):
        src = pl.ds((c // per_class) + (c % per_class) * (4 * QBLK), QBLK, stride=4)
        write_operands(ops_b, kt_b, c, qn[src, :], kn[src, :], v_ref.at[0][src, :])

    qi = lax.broadcasted_iota(jnp.int32, (QBLK, 2 * QBLK), 0)
    kj = lax.broadcasted_iota(jnp.int32, (QBLK, 2 * QBLK), 1)
    bias_full[...] = jnp.where((kj >= qi) & (kj <= qi + QBLK), 0.0, NEG)
    qi1 = lax.broadcasted_iota(jnp.int32, (QBLK, QBLK), 0)
    kj1 = lax.broadcasted_iota(jnp.int32, (QBLK, QBLK), 1)
    bias_first[...] = jnp.where(kj1 <= qi1, 0.0, NEG)
    same4 = ((qi1 - kj1) & 3) == 0
    far = jnp.where(same4, 0.0, NEG)
    prev = jnp.where(same4, jnp.where(kj1 >= qi1, 1.0, 0.0), jnp.where(kj1 >= qi1, 0.0, NEG))
    cur = jnp.where(kj1 <= qi1, jnp.where(same4, 1.0, 0.0), NEG)
    for t, tile in enumerate((far, far, prev, cur)):
        bias_mod4[:, t * QBLK:(t + 1) * QBLK] = tile

    def unit(ops, kt, b, k0, nk, bias, out):
        krows = pl.ds(k0, nk)
        kcat = jnp.concatenate([kt[0, :, k0:k0 + nk], kt[1, :, k0:k0 + nk]], axis=1)
        s2 = jnp.dot(ops[0, blk(b), :], kcat, preferred_element_type=F32)
        yield
        ss, ms = [], []
        for h in range(2):
            s = s2[:, h * nk:(h + 1) * nk] + bias
            ms.append(jnp.max(s, axis=-1, keepdims=True))
            ss.append(s)
        yield
        es = [jnp.exp2(s - m).astype(BF16) for s, m in zip(ss, ms)]
        yield
        rhs = jnp.concatenate([
            jnp.concatenate([ops[1, krows, :], jnp.broadcast_to(one0, (nk, LANES))], axis=1),
            jnp.concatenate([ops[2, krows, :], jnp.broadcast_to(one1, (nk, LANES))], axis=1)], axis=0)
        o = jnp.dot(jnp.concatenate(es, axis=1), rhs, preferred_element_type=F32)
        yield
        out.extend([jnp.where(head0, ms[0], ms[1]), o[:, :LANES], o[:, LANES:]])

    def advance(gens):
        for g in list(gens):
            try:
                next(g)
            except StopIteration:
                gens.remove(g)

    def run_pipelined(tasks):
        pending, active = list(tasks), []
        while pending or active:
            if pending:
                active.append(pending.pop(0))
            advance(active)

    def unit_natural(b):
        part = []
        if b == 0:
            yield from unit(ops_a, kt_a, b, 0, QBLK, bias_first[...], part)
        else:
            yield from unit(ops_a, kt_a, b, (b - 1) * QBLK, 2 * QBLK, bias_full[...], part)
        for k in range(3):
            res[k, blk(b), :] = part[k]
        yield
        to_mod4(b)

    def unit_mod4(b):
        r, i = b // per_class, b % per_class
        rows = pl.ds(r + i * (4 * QBLK), QBLK, stride=4)
        seg = r * per_class * QBLK
        near = min(i + 1, 2)
        part_near, part_far = [], []
        gens = [unit(ops_b, kt_b, b, seg + (i + 1 - near) * QBLK, near * QBLK,
                     bias_mod4[:, (per_class - near) * QBLK:], part_near)]
        if i >= 2:
            gens.append(unit(ops_b, kt_b, b, seg, (i - 1) * QBLK, bias_mod4[:, :(i - 1) * QBLK], part_far))
        while gens:
            advance(gens)
            yield
        parts = [[res.at[k][rows, :] for k in range(3)], part_near] + ([part_far] if part_far else [])
        mx = functools.reduce(jnp.maximum, [p[0] for p in parts])
        ws = [jnp.exp2(p[0] - mx) for p in parts]
        num = functools.reduce(jnp.add, [w * p[1] for w, p in zip(ws, parts)])
        den = functools.reduce(jnp.add, [w * p[2] for w, p in zip(ws, parts)])
        g = ga_ref.at[0][rows, :]
        out32[rows, :] = num / den * (g * _sigmoid(g))

    def cast_out(i):
        o_ref[0, blk(i), :] = out32[blk(i), :].astype(o_ref.dtype)

    for b in range(nblk):
        norm_block(b)
    run_pipelined([unit_natural(b) for b in range(nblk)])
    run_pipelined([unit_mod4(b) for b in range(nblk)])
    for i in range(nblk):
        cast_out(i)


def _attention(q, k, v, gate, q_gain, k_gain):
    b, s, width = q.shape
    pairs = width // LANES
    spec = pl.BlockSpec((1, s, LANES), lambda i, h: (i, 0, h))
    gspec = pl.BlockSpec((1, LANES), lambda i, h: (0, 0))
    tile2 = lambda g: jnp.concatenate([g, g]).reshape(1, LANES)
    return pl.pallas_call(
        _attn_kernel,
        grid=(b, pairs),
        in_specs=[spec, spec, spec, spec, gspec, gspec],
        out_specs=spec,
        out_shape=jax.ShapeDtypeStruct((b, s, width), BF16),
        scratch_shapes=[
            pltpu.VMEM((s, LANES), F32), pltpu.VMEM((s, LANES), F32),
            pltpu.VMEM((3, s, LANES), BF16), pltpu.VMEM((2, LANES, s), BF16),
            pltpu.VMEM((3, s, LANES), BF16), pltpu.VMEM((2, LANES, s), BF16),
            pltpu.VMEM((3, s, LANES), F32),
            pltpu.VMEM((s, LANES), F32),
            pltpu.VMEM((QBLK, 2 * QBLK), F32), pltpu.VMEM((QBLK, QBLK), F32), pltpu.VMEM((QBLK, 4 * QBLK), F32),
        ],
        compiler_params=pltpu.CompilerParams(
            dimension_semantics=("arbitrary", "arbitrary"), vmem_limit_bytes=VMEM_LIMIT),
        name="attn",
    )(q, k, v, gate, tile2(q_gain), tile2(k_gain))


def _ssm_kernel(u_ref, wb_ref, wcr_ref, wci_ref, are_ref, aim_ref, d_ref, y_ref,
                ut, sre, sim, *, batch):
    nslab, nj = u_ref.shape[0], u_ref.shape[1]
    steps = nslab * TIME_TILE

    @pl.when(pl.program_id(0) == 0)
    def _():
        sre[...] = jnp.zeros_like(sre)
        sim[...] = jnp.zeros_like(sim)

    for c in range(nslab):
        for j in range(nj):
            src = u_ref.at[c, j]
            for t in range(TIME_TILE):
                ut[pl.ds((c * TIME_TILE + t) * batch, batch), j * LANES:(j + 1) * LANES] = (
                    src[pl.ds(t, batch, stride=TIME_TILE), :])

    sqrt_half = math.sqrt(0.5)
    for j in range(nj):
        sl = slice(j * STATE_CHUNK, (j + 1) * STATE_CHUNK)
        lanes = slice(j * LANES, (j + 1) * LANES)
        uj = ut[:, lanes]
        bu = jnp.dot(uj.astype(BF16), wb_ref[j], preferred_element_type=F32)
        a_re = jnp.broadcast_to(are_ref[:, sl], (batch, STATE_CHUNK))
        a_im = jnp.broadcast_to(aim_ref[:, sl], (batch, STATE_CHUNK))
        x_re, x_im = sre[:, sl], sim[:, sl]
        hist_re, hist_im = [], []
        for t in range(steps):
            b_re = bu[t * batch:(t + 1) * batch, :STATE_CHUNK]
            b_im = bu[t * batch:(t + 1) * batch, STATE_CHUNK:]
            x_re, x_im = a_re * x_re - a_im * x_im + b_re, a_re * x_im + a_im * x_re + b_im
            hist_re.append(x_re.astype(BF16))
            hist_im.append(x_im.astype(BF16))
        sre[:, sl] = x_re
        sim[:, sl] = x_im
        y = (jnp.dot(jnp.concatenate(hist_re, axis=0), wcr_ref[j], preferred_element_type=F32)
             - jnp.dot(jnp.concatenate(hist_im, axis=0), wci_ref[j], preferred_element_type=F32))
        y = y + d_ref[:, lanes] * uj
        yg = 0.5 * y * (1.0 + lax.erf(y * sqrt_half))
        for c in range(nslab):
            dst = y_ref.at[c, j]
            for t in range(TIME_TILE):
                r0 = (c * TIME_TILE + t) * batch
                dst[pl.ds(t, batch, stride=TIME_TILE), :] = yg[r0:r0 + batch, :]


def _ssm(u_tiles, batch, w_b, w_cre, w_cim, a_re, a_im, d_skip):
    nslab_total, nj, _, _ = u_tiles.shape
    nslab = SSM_STEPS // TIME_TILE
    rows = SSM_STEPS * batch
    nstate = a_re.shape[1]
    tile_spec = pl.BlockSpec((nslab, nj, batch * TIME_TILE, LANES), lambda t: (t, 0, 0, 0))
    full = lambda a: pl.BlockSpec(a.shape, lambda t: (0,) * a.ndim)
    d2 = d_skip.reshape(1, -1)
    return pl.pallas_call(
        functools.partial(_ssm_kernel, batch=batch),
        grid=(nslab_total // nslab,),
        in_specs=[tile_spec, full(w_b), full(w_cre), full(w_cim), full(a_re), full(a_im), full(d2)],
        out_specs=tile_spec,
        out_shape=jax.ShapeDtypeStruct(u_tiles.shape, F32),
        scratch_shapes=[
            pltpu.VMEM((rows, nj * LANES), F32),
            pltpu.VMEM((batch, nstate), F32), pltpu.VMEM((batch, nstate), F32),
        ],
        compiler_params=pltpu.CompilerParams(
            dimension_semantics=("arbitrary",), vmem_limit_bytes=VMEM_LIMIT),
        name="ssm",
    )(u_tiles, w_b, w_cre, w_cim, a_re, a_im, d2)


def _out_kernel(attn_ref, y_ref, gs_ref, x_ref, p_ref, wglu_ref, bglu_ref, wout_ref, gple_ref,
                wg_ref, wp_ref, o_ref):
    nslab, nj = y_ref.shape[0], y_ref.shape[1]
    rows = nslab * TIME_TILE
    aw = attn_ref.shape[-1]
    yg = jnp.concatenate([y_ref[:, j].reshape(rows, LANES) for j in range(nj)], axis=1)
    glu = jnp.dot(yg.astype(BF16), wglu_ref[...], preferred_element_type=F32) + bglu_ref[...]
    gs = gs_ref[0]
    ssm = yg * _sigmoid(glu) * (gs * _sigmoid(gs))
    h = (x_ref[0]
         + jnp.dot(attn_ref[0], wout_ref[:aw, :], preferred_element_type=F32)
         + jnp.dot(ssm.astype(BF16), wout_ref[aw:, :], preferred_element_type=F32))
    ms = jnp.mean(h * h, axis=-1, keepdims=True)
    hn = (h * lax.rsqrt(ms + EPS) * gple_ref[...]).astype(BF16)
    gate = _sigmoid(jnp.dot(hn, wg_ref[...], preferred_element_type=F32))
    pp = jnp.dot(p_ref[0].astype(BF16), wp_ref[...], preferred_element_type=F32)
    o_ref[0] = h + gate * pp


def _out_stage(attn, y_tiles, gate_s, x, p, w_glu, b_glu, w_out, ple_gain, w_gate, w_proj):
    b, s, d = x.shape
    rows = ROW_TILE
    nj = y_tiles.shape[1]
    nat = lambda a: pl.BlockSpec((1, rows, a.shape[-1]), lambda i, t: (i, t, 0))
    full = lambda a: pl.BlockSpec(a.shape, lambda i, t: (0,) * a.ndim)
    y_spec = pl.BlockSpec((rows // TIME_TILE, nj, TIME_TILE, LANES), lambda i, t: (t, 0, i, 0))
    b2 = b_glu.reshape(1, -1)
    g2 = ple_gain.reshape(1, -1)
    return pl.pallas_call(
        _out_kernel,
        grid=(b, s // rows),
        in_specs=[nat(attn), y_spec, nat(gate_s), nat(x), nat(p), full(w_glu), full(b2), full(w_out),
                  full(g2), full(w_gate), full(w_proj)],
        out_specs=nat(x),
        out_shape=jax.ShapeDtypeStruct(x.shape, x.dtype),
        compiler_params=pltpu.CompilerParams(
            dimension_semantics=("arbitrary", "arbitrary"), vmem_limit_bytes=VMEM_LIMIT),
        name="outstage",
    )(attn, y_tiles, gate_s, x, p, w_glu, b2, w_out, g2, w_gate, w_proj)


def kernel(x, p, mix_norm, w_in, q_norm, k_norm, lambda_re, lambda_im, log_dt, b_re, b_im, c_re, c_im,
           d_skip, w_glu, b_glu, w_out, ple_norm, w_ple_gate, w_ple_proj):
    batch = x.shape[0]
    h = x
    for i in range(w_in.shape[0]):
        q, k, v, gate_a, u_tiles, gate_s = _inproj(h, mix_norm[i], w_in[i].astype(BF16))
        a_re, a_im, w_b, w_cre, w_cim = _ssm_params(
            lambda_re[i], lambda_im[i], log_dt[i], b_re[i], b_im[i], c_re[i], c_im[i])
        attn = _attention(q, k, v, gate_a, q_norm[i], k_norm[i])
        y_tiles = _ssm(u_tiles, batch, w_b, w_cre, w_cim, a_re, a_im, d_skip[i])
        h = _out_stage(attn, y_tiles, gate_s, h, p[i], w_glu[i].astype(BF16), b_glu[i],
                       w_out[i].astype(BF16), ple_norm[i], w_ple_gate[i].astype(BF16),
                       w_ple_proj[i].astype(BF16))
    return h
```

```python
import functools
import math

import jax
import jax.numpy as jnp
from jax import lax
from jax.experimental import pallas as pl
from jax.experimental.pallas import tpu as pltpu

F32 = jnp.float32
BF16 = jnp.bfloat16

HEAD_DIM = 64
SSM_GROUP = 16
SSM_STATE = 64
DILATED_CONFIGS = ((128, 1), (512, 4), (2048, 16))
EPS = 1e-6

LANES = 128
QBLK = 128
NEG = -1e30
LOG2E = math.log2(math.e)
ROW_TILE = 512
IN_ROW_TILE = 1024
SSM_STEPS = 32
TIME_TILE = 8
GROUPS_PER_CHUNK = LANES // SSM_GROUP
STATE_CHUNK = GROUPS_PER_CHUNK * SSM_STATE
VMEM_LIMIT = 56 * 1024 * 1024


def _sigmoid(x):
    return jax.nn.sigmoid(x)


def _inproj_kernel(x_ref, g_ref, w_ref, q_ref, k_ref, v_ref, ga_ref, u_ref, gs_ref, *, width):
    x = x_ref[0]
    ms = jnp.mean(x * x, axis=-1, keepdims=True)
    xn = (x * lax.rsqrt(ms + EPS) * g_ref[...]).astype(BF16)

    def proj(idx):
        return jnp.dot(xn, w_ref[:, idx * width:(idx + 1) * width], preferred_element_type=F32)

    q_ref[0] = proj(0)
    k_ref[0] = proj(1)
    v_ref[0] = proj(2)
    ga_ref[0] = proj(3)
    u = proj(4)
    for c in range(u.shape[0] // TIME_TILE):
        for j in range(width // LANES):
            u_ref[c, j] = u[c * TIME_TILE:(c + 1) * TIME_TILE, j * LANES:(j + 1) * LANES]
    gs_ref[0] = proj(5)


def _inproj(x, gain, w_bf16):
    b, s, d = x.shape
    width = w_bf16.shape[1] // 6
    nj = width // LANES
    rows = IN_ROW_TILE
    nat = jax.ShapeDtypeStruct((b, s, width), F32)
    nat_spec = pl.BlockSpec((1, rows, width), lambda i, t: (i, t, 0))
    u_shape = jax.ShapeDtypeStruct((s // TIME_TILE, nj, b * TIME_TILE, LANES), F32)
    u_spec = pl.BlockSpec((rows // TIME_TILE, nj, TIME_TILE, LANES), lambda i, t: (t, 0, i, 0))
    return pl.pallas_call(
        functools.partial(_inproj_kernel, width=width),
        grid=(b, s // rows),
        in_specs=[
            pl.BlockSpec((1, rows, d), lambda i, t: (i, t, 0)),
            pl.BlockSpec((1, d), lambda i, t: (0, 0)),
            pl.BlockSpec(w_bf16.shape, lambda i, t: (0, 0), pipeline_mode=pl.Buffered(1)),
        ],
        out_specs=[nat_spec, nat_spec, nat_spec, nat_spec, u_spec, nat_spec],
        out_shape=[nat, nat, nat, nat, u_shape, nat],
        compiler_params=pltpu.CompilerParams(
            dimension_semantics=("arbitrary", "arbitrary"), vmem_limit_bytes=VMEM_LIMIT),
        name="inproj",
    )(x, gain.reshape(1, d), w_bf16)


def _zoh(lam_re, lam_im, dt):
    mag = jnp.exp(lam_re * dt)
    a_re = mag * jnp.cos(lam_im * dt)
    a_im = mag * jnp.sin(lam_im * dt)
    den = lam_re * lam_re + lam_im * lam_im
    num_re = a_re - 1.0
    coef_re = (num_re * lam_re + a_im * lam_im) / den
    coef_im = (a_im * lam_re - num_re * lam_im) / den
    return a_re, a_im, coef_re, coef_im


def _zoh_kernel(lre_ref, lim_ref, ldt_ref, lre_rep_ref, lim_rep_ref, bre_ref, bim_ref,
                are_ref, aim_ref, bbre_ref, bbim_ref):
    dt = jnp.exp(ldt_ref[...])
    a_re, a_im, _, _ = _zoh(lre_ref[...], lim_ref[...], dt)
    are_ref[...] = a_re
    aim_ref[...] = a_im
    _, _, coef_re, coef_im = _zoh(lre_rep_ref[...], lim_rep_ref[...], dt)
    b_re = bre_ref[...]
    b_im = bim_ref[...]
    bbre_ref[...] = coef_re * b_re - coef_im * b_im
    bbim_ref[...] = coef_re * b_im + coef_im * b_re


def _ssm_params(lam_re, lam_im, log_dt, b_re, b_im, c_re, c_im):
    g, n = lam_re.shape
    c = b_re.shape[-1]
    rep = lambda a: jnp.repeat(a, c, axis=1)
    small = jax.ShapeDtypeStruct((g, n), F32)
    wide = jax.ShapeDtypeStruct((g, n * c), F32)
    a_re, a_im, bb_re, bb_im = pl.pallas_call(
        _zoh_kernel, out_shape=[small, small, wide, wide], name="zoh",
    )(lam_re, lam_im, log_dt.reshape(g, 1), rep(lam_re), rep(lam_im),
      b_re.reshape(g, n * c), b_im.reshape(g, n * c))

    nch = g // GROUPS_PER_CHUNK
    eye = jnp.eye(GROUPS_PER_CHUNK, dtype=bool)

    def in_blockdiag(bb):
        t = bb.reshape(nch, GROUPS_PER_CHUNK, n, c).transpose(0, 1, 3, 2)
        m = jnp.where(eye[None, :, None, :, None], t[:, :, :, None, :], 0.0)
        return m.reshape(nch, GROUPS_PER_CHUNK * c, GROUPS_PER_CHUNK * n)

    def out_blockdiag(cc):
        t = cc.reshape(nch, GROUPS_PER_CHUNK, c, n).transpose(0, 1, 3, 2)
        m = jnp.where(eye[None, :, None, :, None], t[:, :, :, None, :], 0.0)
        return m.reshape(nch, GROUPS_PER_CHUNK * n, GROUPS_PER_CHUNK * c)

    w_b = jnp.concatenate([in_blockdiag(bb_re), in_blockdiag(bb_im)], axis=-1).astype(BF16)
    return (a_re.reshape(1, g * n), a_im.reshape(1, g * n), w_b,
            out_blockdiag(c_re).astype(BF16), out_blockdiag(c_im).astype(BF16))


def _attn_kernel(q_ref, k_ref, v_ref, ga_ref, gq_ref, gk_ref, o_ref,
                 qn, kn, ops_a, kt_a, ops_b, kt_b, res, out32, bias_full, bias_first, bias_mod4):
    seq = qn.shape[0]
    nblk = seq // QBLK
    per_class = nblk // 4
    assert DILATED_CONFIGS == ((128, 1), (512, 4), (2048, 16)) and nblk == 16
    lane = lax.broadcasted_iota(jnp.int32, (1, LANES), 1)
    head0 = lane < HEAD_DIM
    rr = lax.broadcasted_iota(jnp.int32, (2 * LANES, 2 * LANES), 0) // HEAD_DIM
    cc = lax.broadcasted_iota(jnp.int32, (2 * LANES, 2 * LANES), 1) // HEAD_DIM
    head_mean = jnp.where(rr == cc, 1.0 / HEAD_DIM, 0.0).astype(BF16)
    head_mean = jnp.concatenate([head_mean[:LANES], head_mean[:LANES], head_mean[LANES:], head_mean[LANES:]], axis=0)
    dim0 = lax.broadcasted_iota(jnp.int32, (LANES, 1), 0) < HEAD_DIM
    gq = gq_ref[...] * (HEAD_DIM ** -0.5 * LOG2E)
    gk = gk_ref[...]
    one0 = jnp.where(head0, 1.0, 0.0).astype(BF16)
    one1 = jnp.where(head0, 0.0, 1.0).astype(BF16)

    def blk(i, n=QBLK):
        return pl.ds(i * QBLK, n)

    def write_operands(ops, kt, b, qv, kv, vv):
        rows = blk(b)
        vb = vv.astype(BF16)
        zero = jnp.zeros_like(vb)
        ops[0, rows, :] = qv.astype(BF16)
        ops[1, rows, :] = jnp.where(head0, vb, zero)
        ops[2, rows, :] = jnp.where(head0, zero, vb)
        ktb = kv.T.astype(BF16)
        kt[0, :, b * QBLK:(b + 1) * QBLK] = jnp.where(dim0, ktb, zero)
        kt[1, :, b * QBLK:(b + 1) * QBLK] = jnp.where(dim0, zero, ktb)

    def split(x):
        hi = x.astype(BF16)
        return hi, (x - hi.astype(F32)).astype(BF16)

    def norm_block(b):
        rows = blk(b)
        q = q_ref[0, rows, :]
        k = k_ref[0, rows, :]
        ms = jnp.dot(jnp.concatenate(split(q * q) + split(k * k), axis=1), head_mean,
                     preferred_element_type=F32)
        qv = q * lax.rsqrt(ms[:, :LANES] + EPS) * gq
        kv = k * lax.rsqrt(ms[:, LANES:] + EPS) * gk
        qn[rows, :] = qv
        kn[rows, :] = kv
        write_operands(ops_a, kt_a, b, qv, kv, v_ref[0, rows, :])

    def to_mod4(c):
        src = pl.ds((c // per_class) + (c % per_class) * (4 * QBLK), QBLK, stride=4)
        write_operands(ops_b, kt_b, c, qn[src, :], kn[src, :], v_ref.at[0][src, :])

    @pl.when((pl.program_id(0) == 0) & (pl.program_id(1) == 0))
    def _():
        qi = lax.broadcasted_iota(jnp.int32, (QBLK, 2 * QBLK), 0)
        kj = lax.broadcasted_iota(jnp.int32, (QBLK, 2 * QBLK), 1)
        bias_full[...] = jnp.where((kj >= qi) & (kj <= qi + QBLK), 0.0, NEG)
        qi1 = lax.broadcasted_iota(jnp.int32, (QBLK, QBLK), 0)
        kj1 = lax.broadcasted_iota(jnp.int32, (QBLK, QBLK), 1)
        bias_first[...] = jnp.where(kj1 <= qi1, 0.0, NEG)
        same4 = ((qi1 - kj1) & 3) == 0
        far = jnp.where(same4, 0.0, NEG)
        prev = jnp.where(same4, jnp.where(kj1 >= qi1, 1.0, 0.0), jnp.where(kj1 >= qi1, 0.0, NEG))
        cur = jnp.where(kj1 <= qi1, jnp.where(same4, 1.0, 0.0), NEG)
        for t, tile in enumerate((far, far, prev, cur)):
            bias_mod4[:, t * QBLK:(t + 1) * QBLK] = tile

    def unit(ops, kt, b, k0, nk, bias, out):
        krows = pl.ds(k0, nk)
        kcat = jnp.concatenate([kt[0, :, k0:k0 + nk], kt[1, :, k0:k0 + nk]], axis=1)
        s2 = jnp.dot(ops[0, blk(b), :], kcat, preferred_element_type=F32)
        yield
        ss, ms = [], []
        for h in range(2):
            s = s2[:, h * nk:(h + 1) * nk] + bias
            ms.append(jnp.max(s, axis=-1, keepdims=True))
            ss.append(s)
        yield
        es = [jnp.exp2(s - m).astype(BF16) for s, m in zip(ss, ms)]
        yield
        rhs = jnp.concatenate([
            jnp.concatenate([ops[1, krows, :], jnp.broadcast_to(one0, (nk, LANES))], axis=1),
            jnp.concatenate([ops[2, krows, :], jnp.broadcast_to(one1, (nk, LANES))], axis=1)], axis=0)
        o = jnp.dot(jnp.concatenate(es, axis=1), rhs, preferred_element_type=F32)
        yield
        out.extend([jnp.where(head0, ms[0], ms[1]), o[:, :LANES], o[:, LANES:]])

    def advance(gens):
        for g in list(gens):
            try:
                next(g)
            except StopIteration:
                gens.remove(g)

    def run_pipelined(tasks):
        pending, active = list(tasks), []
        while pending or active:
            if pending:
                active.append(pending.pop(0))
            advance(active)

    def unit_natural(b):
        part = []
        if b == 0:
            yield from unit(ops_a, kt_a, b, 0, QBLK, bias_first[...], part)
        else:
            yield from unit(ops_a, kt_a, b, (b - 1) * QBLK, 2 * QBLK, bias_full[...], part)
        for k in range(3):
            res[k, blk(b), :] = part[k]
        yield
        to_mod4(b)

    def unit_mod4(b):
        r, i = b // per_class, b % per_class
        rows = pl.ds(r + i * (4 * QBLK), QBLK, stride=4)
        seg = r * per_class * QBLK
        near = min(i + 1, 2)
        part_near, part_far = [], []
        gens = [unit(ops_b, kt_b, b, seg + (i + 1 - near) * QBLK, near * QBLK,
                     bias_mod4[:, (per_class - near) * QBLK:], part_near)]
        if i >= 2:
            gens.append(unit(ops_b, kt_b, b, seg, (i - 1) * QBLK, bias_mod4[:, :(i - 1) * QBLK], part_far))
        while gens:
            advance(gens)
            yield
        parts = [[res.at[k][rows, :] for k in range(3)], part_near] + ([part_far] if part_far else [])
        mx = functools.reduce(jnp.maximum, [p[0] for p in parts])
        ws = [jnp.exp2(p[0] - mx) for p in parts]
        num = functools.reduce(jnp.add, [w * p[1] for w, p in zip(ws, parts)])
        den = functools.reduce(jnp.add, [w * p[2] for w, p in zip(ws, parts)])
        g = ga_ref.at[0][rows, :]
        out32[rows, :] = num / den * (g * _sigmoid(g))

    def cast_out(i):
        o_ref[0, blk(i), :] = out32[blk(i), :].astype(o_ref.dtype)

    for b in range(nblk):
        norm_block(b)
    run_pipelined([unit_natural(b) for b in range(nblk)])
    run_pipelined([unit_mod4(b) for b in range(nblk)])
    for i in range(nblk):
        cast_out(i)


def _attention(q, k, v, gate, q_gain, k_gain):
    b, s, width = q.shape
    pairs = width // LANES
    spec = pl.BlockSpec((1, s, LANES), lambda i, h: (i, 0, h))
    gspec = pl.BlockSpec((1, LANES), lambda i, h: (0, 0))
    tile2 = lambda g: jnp.concatenate([g, g]).reshape(1, LANES)
    return pl.pallas_call(
        _attn_kernel,
        grid=(b, pairs),
        in_specs=[spec, spec, spec, spec, gspec, gspec],
        out_specs=spec,
        out_shape=jax.ShapeDtypeStruct((b, s, width), BF16),
        scratch_shapes=[
            pltpu.VMEM((s, LANES), F32), pltpu.VMEM((s, LANES), F32),
            pltpu.VMEM((3, s, LANES), BF16), pltpu.VMEM((2, LANES, s), BF16),
            pltpu.VMEM((3, s, LANES), BF16), pltpu.VMEM((2, LANES, s), BF16),
            pltpu.VMEM((3, s, LANES), F32),
            pltpu.VMEM((s, LANES), F32),
            pltpu.VMEM((QBLK, 2 * QBLK), F32), pltpu.VMEM((QBLK, QBLK), F32), pltpu.VMEM((QBLK, 4 * QBLK), F32),
        ],
        compiler_params=pltpu.CompilerParams(
            dimension_semantics=("arbitrary", "arbitrary"), vmem_limit_bytes=VMEM_LIMIT),
        name="attn",
    )(q, k, v, gate, tile2(q_gain), tile2(k_gain))


def _ssm_kernel(u_ref, wb_ref, wcr_ref, wci_ref, are_ref, aim_ref, d_ref, y_ref,
                ut, sre, sim, *, batch):
    nslab, nj = u_ref.shape[0], u_ref.shape[1]
    steps = nslab * TIME_TILE

    @pl.when(pl.program_id(0) == 0)
    def _():
        sre[...] = jnp.zeros_like(sre)
        sim[...] = jnp.zeros_like(sim)

    for c in range(nslab):
        for j in range(nj):
            src = u_ref.at[c, j]
            for t in range(TIME_TILE):
                ut[pl.ds((c * TIME_TILE + t) * batch, batch), j * LANES:(j + 1) * LANES] = (
                    src[pl.ds(t, batch, stride=TIME_TILE), :])

    sqrt_half = math.sqrt(0.5)

    def chunk(j):
        sl = slice(j * STATE_CHUNK, (j + 1) * STATE_CHUNK)
        lanes = slice(j * LANES, (j + 1) * LANES)
        uj = ut[:, lanes]
        bu = jnp.dot(uj.astype(BF16), wb_ref[j], preferred_element_type=F32)
        yield
        a_re = jnp.broadcast_to(are_ref[:, sl], (batch, STATE_CHUNK))
        a_im = jnp.broadcast_to(aim_ref[:, sl], (batch, STATE_CHUNK))
        x_re, x_im = sre[:, sl], sim[:, sl]
        hist_re, hist_im = [], []
        for t in range(steps):
            b_re = bu[t * batch:(t + 1) * batch, :STATE_CHUNK]
            b_im = bu[t * batch:(t + 1) * batch, STATE_CHUNK:]
            x_re, x_im = a_re * x_re - a_im * x_im + b_re, a_re * x_im + a_im * x_re + b_im
            hist_re.append(x_re.astype(BF16))
            hist_im.append(x_im.astype(BF16))
        sre[:, sl] = x_re
        sim[:, sl] = x_im
        yield
        y = (jnp.dot(jnp.concatenate(hist_re, axis=0), wcr_ref[j], preferred_element_type=F32)
             - jnp.dot(jnp.concatenate(hist_im, axis=0), wci_ref[j], preferred_element_type=F32))
        yield
        y = y + d_ref[:, lanes] * uj
        yg = 0.5 * y * (1.0 + lax.erf(y * sqrt_half))
        for c in range(nslab):
            dst = y_ref.at[c, j]
            for t in range(TIME_TILE):
                r0 = (c * TIME_TILE + t) * batch
                dst[pl.ds(t, batch, stride=TIME_TILE), :] = yg[r0:r0 + batch, :]

    pending, active = [chunk(j) for j in range(nj)], []
    while pending or active:
        if pending:
            active.append(pending.pop(0))
        for g in list(active):
            try:
                next(g)
            except StopIteration:
                active.remove(g)


def _ssm(u_tiles, batch, w_b, w_cre, w_cim, a_re, a_im, d_skip):
    nslab_total, nj, _, _ = u_tiles.shape
    nslab = SSM_STEPS // TIME_TILE
    rows = SSM_STEPS * batch
    nstate = a_re.shape[1]
    tile_spec = pl.BlockSpec((nslab, nj, batch * TIME_TILE, LANES), lambda t: (t, 0, 0, 0))
    full = lambda a: pl.BlockSpec(a.shape, lambda t: (0,) * a.ndim)
    d2 = d_skip.reshape(1, -1)
    return pl.pallas_call(
        functools.partial(_ssm_kernel, batch=batch),
        grid=(nslab_total // nslab,),
        in_specs=[tile_spec, full(w_b), full(w_cre), full(w_cim), full(a_re), full(a_im), full(d2)],
        out_specs=tile_spec,
        out_shape=jax.ShapeDtypeStruct(u_tiles.shape, F32),
        scratch_shapes=[
            pltpu.VMEM((rows, nj * LANES), F32),
            pltpu.VMEM((batch, nstate), F32), pltpu.VMEM((batch, nstate), F32),
        ],
        compiler_params=pltpu.CompilerParams(
            dimension_semantics=("arbitrary",), vmem_limit_bytes=VMEM_LIMIT),
        name="ssm",
    )(u_tiles, w_b, w_cre, w_cim, a_re, a_im, d2)


def _out_kernel(attn_ref, y_ref, gs_ref, x_ref, p_ref, wglu_ref, bglu_ref, wout_ref, gple_ref,
                wg_ref, wp_ref, o_ref):
    nslab, nj = y_ref.shape[0], y_ref.shape[1]
    rows = nslab * TIME_TILE
    aw = attn_ref.shape[-1]
    yg = jnp.concatenate([y_ref[:, j].reshape(rows, LANES) for j in range(nj)], axis=1)
    glu = jnp.dot(yg.astype(BF16), wglu_ref[...], preferred_element_type=F32) + bglu_ref[...]
    gs = gs_ref[0]
    ssm = yg * _sigmoid(glu) * (gs * _sigmoid(gs))
    h = (x_ref[0]
         + jnp.dot(attn_ref[0], wout_ref[:aw, :], preferred_element_type=F32)
         + jnp.dot(ssm.astype(BF16), wout_ref[aw:, :], preferred_element_type=F32))
    ms = jnp.mean(h * h, axis=-1, keepdims=True)
    hn = (h * lax.rsqrt(ms + EPS) * gple_ref[...]).astype(BF16)
    gate = _sigmoid(jnp.dot(hn, wg_ref[...], preferred_element_type=F32))
    pp = jnp.dot(p_ref[0].astype(BF16), wp_ref[...], preferred_element_type=F32)
    o_ref[0] = h + gate * pp


def _out_stage(attn, y_tiles, gate_s, x, p, w_glu, b_glu, w_out, ple_gain, w_gate, w_proj):
    b, s, d = x.shape
    rows = ROW_TILE
    nj = y_tiles.shape[1]
    nat = lambda a: pl.BlockSpec((1, rows, a.shape[-1]), lambda i, t: (i, t, 0))
    full = lambda a: pl.BlockSpec(a.shape, lambda i, t: (0,) * a.ndim)
    y_spec = pl.BlockSpec((rows // TIME_TILE, nj, TIME_TILE, LANES), lambda i, t: (t, 0, i, 0))
    b2 = b_glu.reshape(1, -1)
    g2 = ple_gain.reshape(1, -1)
    return pl.pallas_call(
        _out_kernel,
        grid=(b, s // rows),
        in_specs=[nat(attn), y_spec, nat(gate_s), nat(x), nat(p), full(w_glu), full(b2), full(w_out),
                  full(g2), full(w_gate), full(w_proj)],
        out_specs=nat(x),
        out_shape=jax.ShapeDtypeStruct(x.shape, x.dtype),
        compiler_params=pltpu.CompilerParams(
            dimension_semantics=("arbitrary", "arbitrary"), vmem_limit_bytes=VMEM_LIMIT),
        name="outstage",
    )(attn, y_tiles, gate_s, x, p, w_glu, b2, w_out, g2, w_gate, w_proj)


def kernel(x, p, mix_norm, w_in, q_norm, k_norm, lambda_re, lambda_im, log_dt, b_re, b_im, c_re, c_im,
           d_skip, w_glu, b_glu, w_out, ple_norm, w_ple_gate, w_ple_proj):
    batch = x.shape[0]
    h = x
    for i in range(w_in.shape[0]):
        q, k, v, gate_a, u_tiles, gate_s = _inproj(h, mix_norm[i], w_in[i].astype(BF16))
        a_re, a_im, w_b, w_cre, w_cim = _ssm_params(
            lambda_re[i], lambda_im[i], log_dt[i], b_re[i], b_im[i], c_re[i], c_im[i])
        attn = _attention(q, k, v, gate_a, q_norm[i], k_norm[i])
        y_tiles = _ssm(u_tiles, batch, w_b, w_cre, w_cim, a_re, a_im, d_skip[i])
        h = _out_stage(attn, y_tiles, gate_s, h, p[i], w_glu[i].astype(BF16), b_glu[i],
                       w_out[i].astype(BF16), ple_norm[i], w_ple_gate[i].astype(BF16),
                       w_ple_proj[i].astype(BF16))
    return h
```

```python
import functools
import math

import jax
import jax.numpy as jnp
from jax import lax
from jax.experimental import pallas as pl
from jax.experimental.pallas import tpu as pltpu

F32 = jnp.float32
BF16 = jnp.bfloat16

HEAD_DIM = 64
SSM_GROUP = 16
SSM_STATE = 64
DILATED_CONFIGS = ((128, 1), (512, 4), (2048, 16))
EPS = 1e-6

LANES = 128
QBLK = 128
NEG = -1e30
LOG2E = math.log2(math.e)
IN_ROW_TILE = 1024
OUT_ROW_TILE = 1024
OUT_CHUNK_ROWS = 256
SSM_STEPS = 32
TIME_TILE = 8
GROUPS_PER_CHUNK = LANES // SSM_GROUP
STATE_CHUNK = GROUPS_PER_CHUNK * SSM_STATE
VMEM_LIMIT = 56 * 1024 * 1024


def _sigmoid(x):
    return jax.nn.sigmoid(x)


def _advance(gens):
    for g in list(gens):
        try:
            next(g)
        except StopIteration:
            gens.remove(g)


def _interleave(tasks):
    pending, active = list(tasks), []
    while pending or active:
        if pending:
            active.append(pending.pop(0))
        _advance(active)


def _inproj_kernel(x_ref, g_ref, w_ref, q_ref, k_ref, v_ref, ga_ref, u_ref, gs_ref, *, width):
    x = x_ref[0]
    ms = jnp.mean(x * x, axis=-1, keepdims=True)
    xn = (x * lax.rsqrt(ms + EPS) * g_ref[...]).astype(BF16)

    def proj(idx):
        return jnp.dot(xn, w_ref[:, idx * width:(idx + 1) * width], preferred_element_type=F32)

    q_ref[0] = proj(0)
    k_ref[0] = proj(1)
    v_ref[0] = proj(2)
    ga_ref[0] = proj(3)
    u = proj(4)
    for c in range(u.shape[0] // TIME_TILE):
        for j in range(width // LANES):
            u_ref[c, j] = u[c * TIME_TILE:(c + 1) * TIME_TILE, j * LANES:(j + 1) * LANES]
    gs_ref[0] = proj(5)


def _inproj(x, gain, w_bf16):
    b, s, d = x.shape
    width = w_bf16.shape[1] // 6
    nj = width // LANES
    rows = IN_ROW_TILE
    nat = jax.ShapeDtypeStruct((b, s, width), F32)
    nat_spec = pl.BlockSpec((1, rows, width), lambda i, t: (i, t, 0))
    u_shape = jax.ShapeDtypeStruct((s // TIME_TILE, nj, b * TIME_TILE, LANES), F32)
    u_spec = pl.BlockSpec((rows // TIME_TILE, nj, TIME_TILE, LANES), lambda i, t: (t, 0, i, 0))
    return pl.pallas_call(
        functools.partial(_inproj_kernel, width=width),
        grid=(b, s // rows),
        in_specs=[
            pl.BlockSpec((1, rows, d), lambda i, t: (i, t, 0)),
            pl.BlockSpec((1, d), lambda i, t: (0, 0)),
            pl.BlockSpec(w_bf16.shape, lambda i, t: (0, 0), pipeline_mode=pl.Buffered(1)),
        ],
        out_specs=[nat_spec, nat_spec, nat_spec, nat_spec, u_spec, nat_spec],
        out_shape=[nat, nat, nat, nat, u_shape, nat],
        compiler_params=pltpu.CompilerParams(
            dimension_semantics=("arbitrary", "arbitrary"), vmem_limit_bytes=VMEM_LIMIT),
        name="inproj",
    )(x, gain.reshape(1, d), w_bf16)


def _zoh(lam_re, lam_im, dt):
    mag = jnp.exp(lam_re * dt)
    a_re = mag * jnp.cos(lam_im * dt)
    a_im = mag * jnp.sin(lam_im * dt)
    den = lam_re * lam_re + lam_im * lam_im
    num_re = a_re - 1.0
    coef_re = (num_re * lam_re + a_im * lam_im) / den
    coef_im = (a_im * lam_re - num_re * lam_im) / den
    return a_re, a_im, coef_re, coef_im


def _zoh_kernel(lre_ref, lim_ref, ldt_ref, lre_rep_ref, lim_rep_ref, bre_ref, bim_ref,
                are_ref, aim_ref, bbre_ref, bbim_ref):
    dt = jnp.exp(ldt_ref[...])
    a_re, a_im, _, _ = _zoh(lre_ref[...], lim_ref[...], dt)
    are_ref[...] = a_re
    aim_ref[...] = a_im
    _, _, coef_re, coef_im = _zoh(lre_rep_ref[...], lim_rep_ref[...], dt)
    b_re = bre_ref[...]
    b_im = bim_ref[...]
    bbre_ref[...] = coef_re * b_re - coef_im * b_im
    bbim_ref[...] = coef_re * b_im + coef_im * b_re


def _ssm_params(lam_re, lam_im, log_dt, b_re, b_im, c_re, c_im):
    g, n = lam_re.shape
    c = b_re.shape[-1]
    rep = lambda a: jnp.repeat(a, c, axis=1)
    small = jax.ShapeDtypeStruct((g, n), F32)
    wide = jax.ShapeDtypeStruct((g, n * c), F32)
    a_re, a_im, bb_re, bb_im = pl.pallas_call(
        _zoh_kernel, out_shape=[small, small, wide, wide], name="zoh",
    )(lam_re, lam_im, log_dt.reshape(g, 1), rep(lam_re), rep(lam_im),
      b_re.reshape(g, n * c), b_im.reshape(g, n * c))

    nch = g // GROUPS_PER_CHUNK
    eye = jnp.eye(GROUPS_PER_CHUNK, dtype=bool)

    def in_blockdiag(bb):
        t = bb.reshape(nch, GROUPS_PER_CHUNK, n, c).transpose(0, 1, 3, 2)
        m = jnp.where(eye[None, :, None, :, None], t[:, :, :, None, :], 0.0)
        return m.reshape(nch, GROUPS_PER_CHUNK * c, GROUPS_PER_CHUNK * n)

    def out_blockdiag(cc):
        t = cc.reshape(nch, GROUPS_PER_CHUNK, c, n).transpose(0, 1, 3, 2)
        m = jnp.where(eye[None, :, None, :, None], t[:, :, :, None, :], 0.0)
        return m.reshape(nch, GROUPS_PER_CHUNK * n, GROUPS_PER_CHUNK * c)

    w_b = jnp.concatenate([in_blockdiag(bb_re), in_blockdiag(bb_im)], axis=-1).astype(BF16)
    return (a_re.reshape(1, g * n), a_im.reshape(1, g * n), w_b,
            out_blockdiag(c_re).astype(BF16), out_blockdiag(c_im).astype(BF16))


def _attn_kernel(q_ref, k_ref, v_ref, ga_ref, gq_ref, gk_ref, o_ref,
                 qn, kn, ops_a, kt_a, ops_b, kt_b, res, out32, bias_full, bias_first, bias_mod4):
    seq = qn.shape[0]
    nblk = seq // QBLK
    per_class = nblk // 4
    assert DILATED_CONFIGS == ((128, 1), (512, 4), (2048, 16)) and nblk == 16
    lane = lax.broadcasted_iota(jnp.int32, (1, LANES), 1)
    head0 = lane < HEAD_DIM
    rr = lax.broadcasted_iota(jnp.int32, (2 * LANES, 2 * LANES), 0) // HEAD_DIM
    cc = lax.broadcasted_iota(jnp.int32, (2 * LANES, 2 * LANES), 1) // HEAD_DIM
    head_mean = jnp.where(rr == cc, 1.0 / HEAD_DIM, 0.0).astype(BF16)
    head_mean = jnp.concatenate([head_mean[:LANES], head_mean[:LANES], head_mean[LANES:], head_mean[LANES:]], axis=0)
    dim0 = lax.broadcasted_iota(jnp.int32, (LANES, 1), 0) < HEAD_DIM
    gq = gq_ref[...] * (HEAD_DIM ** -0.5 * LOG2E)
    gk = gk_ref[...]
    one0 = jnp.where(head0, 1.0, 0.0).astype(BF16)
    one1 = jnp.where(head0, 0.0, 1.0).astype(BF16)

    def blk(i, n=QBLK):
        return pl.ds(i * QBLK, n)

    def write_operands(ops, kt, b, qv, kv, vv):
        rows = blk(b)
        vb = vv.astype(BF16)
        zero = jnp.zeros_like(vb)
        ops[0, rows, :] = qv.astype(BF16)
        ops[1, rows, :] = jnp.where(head0, vb, zero)
        ops[2, rows, :] = jnp.where(head0, zero, vb)
        ktb = kv.T.astype(BF16)
        kt[0, :, b * QBLK:(b + 1) * QBLK] = jnp.where(dim0, ktb, zero)
        kt[1, :, b * QBLK:(b + 1) * QBLK] = jnp.where(dim0, zero, ktb)

    def split(x):
        hi = x.astype(BF16)
        return hi, (x - hi.astype(F32)).astype(BF16)

    def norm_block(b):
        rows = blk(b)
        q = q_ref[0, rows, :]
        k = k_ref[0, rows, :]
        ms = jnp.dot(jnp.concatenate(split(q * q) + split(k * k), axis=1), head_mean,
                     preferred_element_type=F32)
        qv = q * lax.rsqrt(ms[:, :LANES] + EPS) * gq
        kv = k * lax.rsqrt(ms[:, LANES:] + EPS) * gk
        qn[rows, :] = qv
        kn[rows, :] = kv
        write_operands(ops_a, kt_a, b, qv, kv, v_ref[0, rows, :])

    def to_mod4(c):
        src = pl.ds((c // per_class) + (c % per_class) * (4 * QBLK), QBLK, stride=4)
        write_operands(ops_b, kt_b, c, qn[src, :], kn[src, :], v_ref.at[0][src, :])

    @pl.when((pl.program_id(0) == 0) & (pl.program_id(1) == 0))
    def _():
        qi = lax.broadcasted_iota(jnp.int32, (QBLK, 2 * QBLK), 0)
        kj = lax.broadcasted_iota(jnp.int32, (QBLK, 2 * QBLK), 1)
        bias_full[...] = jnp.where((kj >= qi) & (kj <= qi + QBLK), 0.0, NEG)
        qi1 = lax.broadcasted_iota(jnp.int32, (QBLK, QBLK), 0)
        kj1 = lax.broadcasted_iota(jnp.int32, (QBLK, QBLK), 1)
        bias_first[...] = jnp.where(kj1 <= qi1, 0.0, NEG)
        same4 = ((qi1 - kj1) & 3) == 0
        far = jnp.where(same4, 0.0, NEG)
        prev = jnp.where(same4, jnp.where(kj1 >= qi1, 1.0, 0.0), jnp.where(kj1 >= qi1, 0.0, NEG))
        cur = jnp.where(kj1 <= qi1, jnp.where(same4, 1.0, 0.0), NEG)
        for t, tile in enumerate((far, far, prev, cur)):
            bias_mod4[:, t * QBLK:(t + 1) * QBLK] = tile

    def unit(ops, kt, b, k0, nk, bias, out):
        krows = pl.ds(k0, nk)
        kcat = jnp.concatenate([kt[0, :, k0:k0 + nk], kt[1, :, k0:k0 + nk]], axis=1)
        s2 = jnp.dot(ops[0, blk(b), :], kcat, preferred_element_type=F32)
        yield
        ss, ms = [], []
        for h in range(2):
            s = s2[:, h * nk:(h + 1) * nk] + bias
            ms.append(jnp.max(s, axis=-1, keepdims=True))
            ss.append(s)
        yield
        es = [jnp.exp2(s - m).astype(BF16) for s, m in zip(ss, ms)]
        yield
        rhs = jnp.concatenate([
            jnp.concatenate([ops[1, krows, :], jnp.broadcast_to(one0, (nk, LANES))], axis=1),
            jnp.concatenate([ops[2, krows, :], jnp.broadcast_to(one1, (nk, LANES))], axis=1)], axis=0)
        o = jnp.dot(jnp.concatenate(es, axis=1), rhs, preferred_element_type=F32)
        yield
        out.extend([jnp.where(head0, ms[0], ms[1]), o[:, :LANES], o[:, LANES:]])

    def unit_natural(b):
        part = []
        if b == 0:
            yield from unit(ops_a, kt_a, b, 0, QBLK, bias_first[...], part)
        else:
            yield from unit(ops_a, kt_a, b, (b - 1) * QBLK, 2 * QBLK, bias_full[...], part)
        for k in range(3):
            res[k, blk(b), :] = part[k]
        yield
        to_mod4(b)

    def unit_mod4(b):
        r, i = b // per_class, b % per_class
        rows = pl.ds(r + i * (4 * QBLK), QBLK, stride=4)
        seg = r * per_class * QBLK
        near = min(i + 1, 2)
        part_near, part_far = [], []
        gens = [unit(ops_b, kt_b, b, seg + (i + 1 - near) * QBLK, near * QBLK,
                     bias_mod4[:, (per_class - near) * QBLK:], part_near)]
        if i >= 2:
            gens.append(unit(ops_b, kt_b, b, seg, (i - 1) * QBLK, bias_mod4[:, :(i - 1) * QBLK], part_far))
        while gens:
            _advance(gens)
            yield
        parts = [[res.at[k][rows, :] for k in range(3)], part_near] + ([part_far] if part_far else [])
        mx = functools.reduce(jnp.maximum, [p[0] for p in parts])
        ws = [jnp.exp2(p[0] - mx) for p in parts]
        num = functools.reduce(jnp.add, [w * p[1] for w, p in zip(ws, parts)])
        den = functools.reduce(jnp.add, [w * p[2] for w, p in zip(ws, parts)])
        g = ga_ref.at[0][rows, :]
        out32[rows, :] = num / den * (g * _sigmoid(g))

    def cast_out(i):
        o_ref[0, blk(i), :] = out32[blk(i), :].astype(o_ref.dtype)

    for b in range(nblk):
        norm_block(b)
    _interleave([unit_natural(b) for b in range(nblk)])
    _interleave([unit_mod4(b) for b in range(nblk)])
    for i in range(nblk):
        cast_out(i)


def _attention(q, k, v, gate, q_gain, k_gain):
    b, s, width = q.shape
    pairs = width // LANES
    spec = pl.BlockSpec((1, s, LANES), lambda i, h: (i, 0, h))
    gspec = pl.BlockSpec((1, LANES), lambda i, h: (0, 0))
    tile2 = lambda g: jnp.concatenate([g, g]).reshape(1, LANES)
    return pl.pallas_call(
        _attn_kernel,
        grid=(b, pairs),
        in_specs=[spec, spec, spec, spec, gspec, gspec],
        out_specs=spec,
        out_shape=jax.ShapeDtypeStruct((b, s, width), BF16),
        scratch_shapes=[
            pltpu.VMEM((s, LANES), F32), pltpu.VMEM((s, LANES), F32),
            pltpu.VMEM((3, s, LANES), BF16), pltpu.VMEM((2, LANES, s), BF16),
            pltpu.VMEM((3, s, LANES), BF16), pltpu.VMEM((2, LANES, s), BF16),
            pltpu.VMEM((3, s, LANES), F32),
            pltpu.VMEM((s, LANES), F32),
            pltpu.VMEM((QBLK, 2 * QBLK), F32), pltpu.VMEM((QBLK, QBLK), F32), pltpu.VMEM((QBLK, 4 * QBLK), F32),
        ],
        compiler_params=pltpu.CompilerParams(
            dimension_semantics=("arbitrary", "arbitrary"), vmem_limit_bytes=VMEM_LIMIT),
        name="attn",
    )(q, k, v, gate, tile2(q_gain), tile2(k_gain))


def _ssm_kernel(u_ref, wb_ref, wcr_ref, wci_ref, are_ref, aim_ref, d_ref, y_ref,
                ut, sre, sim, *, batch):
    nslab, nj = u_ref.shape[0], u_ref.shape[1]
    steps = nslab * TIME_TILE

    @pl.when(pl.program_id(0) == 0)
    def _():
        sre[...] = jnp.zeros_like(sre)
        sim[...] = jnp.zeros_like(sim)

    for c in range(nslab):
        for j in range(nj):
            src = u_ref.at[c, j]
            for t in range(TIME_TILE):
                ut[pl.ds((c * TIME_TILE + t) * batch, batch), j * LANES:(j + 1) * LANES] = (
                    src[pl.ds(t, batch, stride=TIME_TILE), :])

    sqrt_half = math.sqrt(0.5)

    def chunk(j):
        sl = slice(j * STATE_CHUNK, (j + 1) * STATE_CHUNK)
        lanes = slice(j * LANES, (j + 1) * LANES)
        uj = ut[:, lanes]
        bu = jnp.dot(uj.astype(BF16), wb_ref[j], preferred_element_type=F32)
        yield
        a_re = jnp.broadcast_to(are_ref[:, sl], (batch, STATE_CHUNK))
        a_im = jnp.broadcast_to(aim_ref[:, sl], (batch, STATE_CHUNK))
        x_re, x_im = sre[:, sl], sim[:, sl]
        hist_re, hist_im = [], []
        for t in range(steps):
            b_re = bu[t * batch:(t + 1) * batch, :STATE_CHUNK]
            b_im = bu[t * batch:(t + 1) * batch, STATE_CHUNK:]
            x_re, x_im = a_re * x_re - a_im * x_im + b_re, a_re * x_im + a_im * x_re + b_im
            hist_re.append(x_re.astype(BF16))
            hist_im.append(x_im.astype(BF16))
        sre[:, sl] = x_re
        sim[:, sl] = x_im
        yield
        y = (jnp.dot(jnp.concatenate(hist_re, axis=0), wcr_ref[j], preferred_element_type=F32)
             - jnp.dot(jnp.concatenate(hist_im, axis=0), wci_ref[j], preferred_element_type=F32))
        yield
        y = y + d_ref[:, lanes] * uj
        yg = 0.5 * y * (1.0 + lax.erf(y * sqrt_half))
        for c in range(nslab):
            dst = y_ref.at[c, j]
            for t in range(TIME_TILE):
                r0 = (c * TIME_TILE + t) * batch
                dst[pl.ds(t, batch, stride=TIME_TILE), :] = yg[r0:r0 + batch, :]

    _interleave([chunk(j) for j in range(nj)])


def _ssm(u_tiles, batch, w_b, w_cre, w_cim, a_re, a_im, d_skip):
    nslab_total, nj, _, _ = u_tiles.shape
    nslab = SSM_STEPS // TIME_TILE
    rows = SSM_STEPS * batch
    nstate = a_re.shape[1]
    tile_spec = pl.BlockSpec((nslab, nj, batch * TIME_TILE, LANES), lambda t: (t, 0, 0, 0))
    full = lambda a: pl.BlockSpec(a.shape, lambda t: (0,) * a.ndim)
    d2 = d_skip.reshape(1, -1)
    return pl.pallas_call(
        functools.partial(_ssm_kernel, batch=batch),
        grid=(nslab_total // nslab,),
        in_specs=[tile_spec, full(w_b), full(w_cre), full(w_cim), full(a_re), full(a_im), full(d2)],
        out_specs=tile_spec,
        out_shape=jax.ShapeDtypeStruct(u_tiles.shape, F32),
        scratch_shapes=[
            pltpu.VMEM((rows, nj * LANES), F32),
            pltpu.VMEM((batch, nstate), F32), pltpu.VMEM((batch, nstate), F32),
        ],
        compiler_params=pltpu.CompilerParams(
            dimension_semantics=("arbitrary",), vmem_limit_bytes=VMEM_LIMIT),
        name="ssm",
    )(u_tiles, w_b, w_cre, w_cim, a_re, a_im, d2)


def _out_kernel(attn_ref, y_ref, gs_ref, x_ref, p_ref, wglu_ref, bglu_ref, wout_ref, gple_ref,
                wg_ref, wp_ref, o_ref):
    nslab, nj = y_ref.shape[0], y_ref.shape[1]
    rows = OUT_CHUNK_ROWS
    aw = attn_ref.shape[-1]

    def chunk(c):
        r = pl.ds(c * rows, rows)
        slabs = slice(c * rows // TIME_TILE, (c + 1) * rows // TIME_TILE)
        yg = jnp.concatenate([y_ref[slabs, j].reshape(rows, LANES) for j in range(nj)], axis=1)
        glu = jnp.dot(yg.astype(BF16), wglu_ref[...], preferred_element_type=F32) + bglu_ref[...]
        pp = jnp.dot(p_ref[0, r, :].astype(BF16), wp_ref[...], preferred_element_type=F32)
        yield
        gs = gs_ref[0, r, :]
        ssm = yg * _sigmoid(glu) * (gs * _sigmoid(gs))
        h = (x_ref[0, r, :]
             + jnp.dot(attn_ref[0, r, :], wout_ref[:aw, :], preferred_element_type=F32)
             + jnp.dot(ssm.astype(BF16), wout_ref[aw:, :], preferred_element_type=F32))
        yield
        ms = jnp.mean(h * h, axis=-1, keepdims=True)
        hn = (h * lax.rsqrt(ms + EPS) * gple_ref[...]).astype(BF16)
        gate = _sigmoid(jnp.dot(hn, wg_ref[...], preferred_element_type=F32))
        yield
        o_ref[0, r, :] = h + gate * pp

    _interleave([chunk(c) for c in range(nslab * TIME_TILE // rows)])


def _out_stage(attn, y_tiles, gate_s, x, p, w_glu, b_glu, w_out, ple_gain, w_gate, w_proj):
    b, s, d = x.shape
    rows = OUT_ROW_TILE
    nj = y_tiles.shape[1]
    nat = lambda a: pl.BlockSpec((1, rows, a.shape[-1]), lambda i, t: (i, t, 0))
    full = lambda a: pl.BlockSpec(a.shape, lambda i, t: (0,) * a.ndim)
    y_spec = pl.BlockSpec((rows // TIME_TILE, nj, TIME_TILE, LANES), lambda i, t: (t, 0, i, 0))
    b2 = b_glu.reshape(1, -1)
    g2 = ple_gain.reshape(1, -1)
    return pl.pallas_call(
        _out_kernel,
        grid=(b, s // rows),
        in_specs=[nat(attn), y_spec, nat(gate_s), nat(x), nat(p), full(w_glu), full(b2), full(w_out),
                  full(g2), full(w_gate), full(w_proj)],
        out_specs=nat(x),
        out_shape=jax.ShapeDtypeStruct(x.shape, x.dtype),
        compiler_params=pltpu.CompilerParams(
            dimension_semantics=("arbitrary", "arbitrary"), vmem_limit_bytes=VMEM_LIMIT),
        name="outstage",
    )(attn, y_tiles, gate_s, x, p, w_glu, b2, w_out, g2, w_gate, w_proj)


def kernel(x, p, mix_norm, w_in, q_norm, k_norm, lambda_re, lambda_im, log_dt, b_re, b_im, c_re, c_im,
           d_skip, w_glu, b_glu, w_out, ple_norm, w_ple_gate, w_ple_proj):
    batch = x.shape[0]
    h = x
    for i in range(w_in.shape[0]):
        q, k, v, gate_a, u_tiles, gate_s = _inproj(h, mix_norm[i], w_in[i].astype(BF16))
        a_re, a_im, w_b, w_cre, w_cim = _ssm_params(
            lambda_re[i], lambda_im[i], log_dt[i], b_re[i], b_im[i], c_re[i], c_im[i])
        attn = _attention(q, k, v, gate_a, q_norm[i], k_norm[i])
        y_tiles = _ssm(u_tiles, batch, w_b, w_cre, w_cim, a_re, a_im, d_skip[i])
        h = _out_stage(attn, y_tiles, gate_s, h, p[i], w_glu[i].astype(BF16), b_glu[i],
                       w_out[i].astype(BF16), ple_norm[i], w_ple_gate[i].astype(BF16),
                       w_ple_proj[i].astype(BF16))
    return h
```

```python
import functools
import math

import jax
import jax.numpy as jnp
from jax import lax
from jax.experimental import pallas as pl
from jax.experimental.pallas import tpu as pltpu

F32 = jnp.float32
BF16 = jnp.bfloat16

HEAD_DIM = 64
SSM_GROUP = 16
SSM_STATE = 64
DILATED_CONFIGS = ((128, 1), (512, 4), (2048, 16))
EPS = 1e-6

LANES = 128
QBLK = 128
NEG = -1e30
LOG2E = math.log2(math.e)
IN_ROW_TILE = 1024
OUT_ROW_TILE = 1024
OUT_CHUNK_ROWS = 256
SSM_STEPS = 32
TIME_TILE = 8
GROUPS_PER_CHUNK = LANES // SSM_GROUP
STATE_CHUNK = GROUPS_PER_CHUNK * SSM_STATE
VMEM_LIMIT = 56 * 1024 * 1024


def _sigmoid(x):
    return jax.nn.sigmoid(x)


def _advance(gens):
    for g in list(gens):
        try:
            next(g)
        except StopIteration:
            gens.remove(g)


def _interleave(tasks):
    pending, active = list(tasks), []
    while pending or active:
        if pending:
            active.append(pending.pop(0))
        _advance(active)


def _inproj_kernel(x_ref, g_ref, w_ref, q_ref, k_ref, v_ref, ga_ref, u_ref, gs_ref, *, width):
    x = x_ref[0]
    ms = jnp.mean(x * x, axis=-1, keepdims=True)
    xn = (x * lax.rsqrt(ms + EPS) * g_ref[...]).astype(BF16)

    def proj(idx):
        return jnp.dot(xn, w_ref[:, idx * width:(idx + 1) * width], preferred_element_type=F32)

    q_ref[0] = proj(0)
    k_ref[0] = proj(1)
    v_ref[0] = proj(2)
    ga_ref[0] = proj(3)
    u = proj(4)
    for c in range(u.shape[0] // TIME_TILE):
        for j in range(width // LANES):
            u_ref[c, j] = u[c * TIME_TILE:(c + 1) * TIME_TILE, j * LANES:(j + 1) * LANES]
    gs_ref[0] = proj(5)


def _inproj(x, gain, w_bf16):
    b, s, d = x.shape
    width = w_bf16.shape[1] // 6
    nj = width // LANES
    rows = IN_ROW_TILE
    nat = jax.ShapeDtypeStruct((b, s, width), F32)
    nat_spec = pl.BlockSpec((1, rows, width), lambda i, t: (i, t, 0))
    u_shape = jax.ShapeDtypeStruct((s // TIME_TILE, nj, b * TIME_TILE, LANES), F32)
    u_spec = pl.BlockSpec((rows // TIME_TILE, nj, TIME_TILE, LANES), lambda i, t: (t, 0, i, 0))
    return pl.pallas_call(
        functools.partial(_inproj_kernel, width=width),
        grid=(b, s // rows),
        in_specs=[
            pl.BlockSpec((1, rows, d), lambda i, t: (i, t, 0)),
            pl.BlockSpec((1, d), lambda i, t: (0, 0)),
            pl.BlockSpec(w_bf16.shape, lambda i, t: (0, 0), pipeline_mode=pl.Buffered(1)),
        ],
        out_specs=[nat_spec, nat_spec, nat_spec, nat_spec, u_spec, nat_spec],
        out_shape=[nat, nat, nat, nat, u_shape, nat],
        compiler_params=pltpu.CompilerParams(
            dimension_semantics=("arbitrary", "arbitrary"), vmem_limit_bytes=VMEM_LIMIT),
        name="inproj",
    )(x, gain.reshape(1, d), w_bf16)


def _zoh(lam_re, lam_im, dt):
    mag = jnp.exp(lam_re * dt)
    a_re = mag * jnp.cos(lam_im * dt)
    a_im = mag * jnp.sin(lam_im * dt)
    den = lam_re * lam_re + lam_im * lam_im
    num_re = a_re - 1.0
    coef_re = (num_re * lam_re + a_im * lam_im) / den
    coef_im = (a_im * lam_re - num_re * lam_im) / den
    return a_re, a_im, coef_re, coef_im


def _zoh_kernel(lre_ref, lim_ref, ldt_ref, lre_rep_ref, lim_rep_ref, bre_ref, bim_ref,
                are_ref, aim_ref, bbre_ref, bbim_ref):
    dt = jnp.exp(ldt_ref[...])
    a_re, a_im, _, _ = _zoh(lre_ref[...], lim_ref[...], dt)
    are_ref[...] = a_re
    aim_ref[...] = a_im
    _, _, coef_re, coef_im = _zoh(lre_rep_ref[...], lim_rep_ref[...], dt)
    b_re = bre_ref[...]
    b_im = bim_ref[...]
    bbre_ref[...] = coef_re * b_re - coef_im * b_im
    bbim_ref[...] = coef_re * b_im + coef_im * b_re


def _ssm_params(lam_re, lam_im, log_dt, b_re, b_im, c_re, c_im):
    g, n = lam_re.shape
    c = b_re.shape[-1]
    rep = lambda a: jnp.repeat(a, c, axis=1)
    small = jax.ShapeDtypeStruct((g, n), F32)
    wide = jax.ShapeDtypeStruct((g, n * c), F32)
    a_re, a_im, bb_re, bb_im = pl.pallas_call(
        _zoh_kernel, out_shape=[small, small, wide, wide], name="zoh",
    )(lam_re, lam_im, log_dt.reshape(g, 1), rep(lam_re), rep(lam_im),
      b_re.reshape(g, n * c), b_im.reshape(g, n * c))

    nch = g // GROUPS_PER_CHUNK
    eye = jnp.eye(GROUPS_PER_CHUNK, dtype=bool)

    def in_blockdiag(bb):
        t = bb.reshape(nch, GROUPS_PER_CHUNK, n, c).transpose(0, 1, 3, 2)
        m = jnp.where(eye[None, :, None, :, None], t[:, :, :, None, :], 0.0)
        return m.reshape(nch, GROUPS_PER_CHUNK * c, GROUPS_PER_CHUNK * n)

    def out_blockdiag(cc):
        t = cc.reshape(nch, GROUPS_PER_CHUNK, c, n).transpose(0, 1, 3, 2)
        m = jnp.where(eye[None, :, None, :, None], t[:, :, :, None, :], 0.0)
        return m.reshape(nch, GROUPS_PER_CHUNK * n, GROUPS_PER_CHUNK * c)

    w_b = jnp.concatenate([in_blockdiag(bb_re), in_blockdiag(bb_im)], axis=-1).astype(BF16)
    return (a_re.reshape(1, g * n), a_im.reshape(1, g * n), w_b,
            out_blockdiag(c_re).astype(BF16), out_blockdiag(c_im).astype(BF16))


def _attn_kernel(q_ref, k_ref, v_ref, ga_ref, gq_ref, gk_ref, o_ref,
                 qn, kn, ops_a, ops_b, res, out32, bias_full, bias_first, bias_mod4):
    seq = qn.shape[0]
    nblk = seq // QBLK
    per_class = nblk // 4
    assert DILATED_CONFIGS == ((128, 1), (512, 4), (2048, 16)) and nblk == 16
    lane = lax.broadcasted_iota(jnp.int32, (1, LANES), 1)
    head0 = lane < HEAD_DIM
    rr = lax.broadcasted_iota(jnp.int32, (2 * LANES, 2 * LANES), 0) // HEAD_DIM
    cc = lax.broadcasted_iota(jnp.int32, (2 * LANES, 2 * LANES), 1) // HEAD_DIM
    head_mean = jnp.where(rr == cc, 1.0 / HEAD_DIM, 0.0).astype(BF16)
    head_mean = jnp.concatenate([head_mean[:LANES], head_mean[:LANES], head_mean[LANES:], head_mean[LANES:]], axis=0)
    gq = gq_ref[...] * (HEAD_DIM ** -0.5 * LOG2E)
    gk = gk_ref[...]
    one0 = jnp.where(head0, 1.0, 0.0).astype(BF16)
    one1 = jnp.where(head0, 0.0, 1.0).astype(BF16)

    def blk(i, n=QBLK):
        return pl.ds(i * QBLK, n)

    def write_operands(ops, b, qv, kv, vv):
        rows = blk(b)
        vb = vv.astype(BF16)
        kb = kv.astype(BF16)
        zero = jnp.zeros_like(vb)
        ops[0, rows, :] = qv.astype(BF16)
        ops[1, rows, :] = jnp.where(head0, vb, zero)
        ops[2, rows, :] = jnp.where(head0, zero, vb)
        ops[3, rows, :] = jnp.where(head0, kb, zero)
        ops[4, rows, :] = jnp.where(head0, zero, kb)

    def split(x):
        hi = x.astype(BF16)
        return hi, (x - hi.astype(F32)).astype(BF16)

    def norm_block(b):
        rows = blk(b)
        q = q_ref[0, rows, :]
        k = k_ref[0, rows, :]
        ms = jnp.dot(jnp.concatenate(split(q * q) + split(k * k), axis=1), head_mean,
                     preferred_element_type=F32)
        qv = q * lax.rsqrt(ms[:, :LANES] + EPS) * gq
        kv = k * lax.rsqrt(ms[:, LANES:] + EPS) * gk
        qn[rows, :] = qv
        kn[rows, :] = kv
        write_operands(ops_a, b, qv, kv, v_ref[0, rows, :])

    def to_mod4(c):
        src = pl.ds((c // per_class) + (c % per_class) * (4 * QBLK), QBLK, stride=4)
        write_operands(ops_b, c, qn[src, :], kn[src, :], v_ref.at[0][src, :])

    @pl.when((pl.program_id(0) == 0) & (pl.program_id(1) == 0))
    def _():
        qi = lax.broadcasted_iota(jnp.int32, (QBLK, 2 * QBLK), 0)
        kj = lax.broadcasted_iota(jnp.int32, (QBLK, 2 * QBLK), 1)
        bias_full[...] = jnp.where((kj >= qi) & (kj <= qi + QBLK), 0.0, NEG)
        qi1 = lax.broadcasted_iota(jnp.int32, (QBLK, QBLK), 0)
        kj1 = lax.broadcasted_iota(jnp.int32, (QBLK, QBLK), 1)
        bias_first[...] = jnp.where(kj1 <= qi1, 0.0, NEG)
        same4 = ((qi1 - kj1) & 3) == 0
        far = jnp.where(same4, 0.0, NEG)
        prev = jnp.where(same4, jnp.where(kj1 >= qi1, 1.0, 0.0), jnp.where(kj1 >= qi1, 0.0, NEG))
        cur = jnp.where(kj1 <= qi1, jnp.where(same4, 1.0, 0.0), NEG)
        for t, tile in enumerate((far, far, prev, cur)):
            bias_mod4[:, t * QBLK:(t + 1) * QBLK] = tile

    def unit(ops, b, k0, nk, bias, out):
        krows = pl.ds(k0, nk)
        kcat = jnp.concatenate([ops[3, krows, :], ops[4, krows, :]], axis=0)
        s2 = lax.dot_general(ops[0, blk(b), :], kcat, (((1,), (1,)), ((), ())),
                             preferred_element_type=F32)
        yield
        ss, ms = [], []
        for h in range(2):
            s = s2[:, h * nk:(h + 1) * nk] + bias
            ms.append(jnp.max(s, axis=-1, keepdims=True))
            ss.append(s)
        yield
        es = [jnp.exp2(s - m).astype(BF16) for s, m in zip(ss, ms)]
        yield
        rhs = jnp.concatenate([
            jnp.concatenate([ops[1, krows, :], jnp.broadcast_to(one0, (nk, LANES))], axis=1),
            jnp.concatenate([ops[2, krows, :], jnp.broadcast_to(one1, (nk, LANES))], axis=1)], axis=0)
        o = jnp.dot(jnp.concatenate(es, axis=1), rhs, preferred_element_type=F32)
        yield
        out.extend([jnp.where(head0, ms[0], ms[1]), o[:, :LANES], o[:, LANES:]])

    def unit_natural(b):
        part = []
        if b == 0:
            yield from unit(ops_a, b, 0, QBLK, bias_first[...], part)
        else:
            yield from unit(ops_a, b, (b - 1) * QBLK, 2 * QBLK, bias_full[...], part)
        for k in range(3):
            res[k, blk(b), :] = part[k]
        yield
        to_mod4(b)

    def unit_mod4(b):
        r, i = b // per_class, b % per_class
        rows = pl.ds(r + i * (4 * QBLK), QBLK, stride=4)
        seg = r * per_class * QBLK
        near = min(i + 1, 2)
        part_near, part_far = [], []
        gens = [unit(ops_b, b, seg + (i + 1 - near) * QBLK, near * QBLK,
                     bias_mod4[:, (per_class - near) * QBLK:], part_near)]
        if i >= 2:
            gens.append(unit(ops_b, b, seg, (i - 1) * QBLK, bias_mod4[:, :(i - 1) * QBLK], part_far))
        while gens:
            _advance(gens)
            yield
        parts = [[res.at[k][rows, :] for k in range(3)], part_near] + ([part_far] if part_far else [])
        mx = functools.reduce(jnp.maximum, [p[0] for p in parts])
        ws = [jnp.exp2(p[0] - mx) for p in parts]
        num = functools.reduce(jnp.add, [w * p[1] for w, p in zip(ws, parts)])
        den = functools.reduce(jnp.add, [w * p[2] for w, p in zip(ws, parts)])
        g = ga_ref.at[0][rows, :]
        out32[rows, :] = num / den * (g * _sigmoid(g))

    def cast_out(i):
        o_ref[0, blk(i), :] = out32[blk(i), :].astype(o_ref.dtype)

    for b in range(nblk):
        norm_block(b)
    _interleave([unit_natural(b) for b in range(nblk)])
    _interleave([unit_mod4(b) for b in range(nblk)])
    for i in range(nblk):
        cast_out(i)


def _attention(q, k, v, gate, q_gain, k_gain):
    b, s, width = q.shape
    pairs = width // LANES
    spec = pl.BlockSpec((1, s, LANES), lambda i, h: (i, 0, h))
    gspec = pl.BlockSpec((1, LANES), lambda i, h: (0, 0))
    tile2 = lambda g: jnp.concatenate([g, g]).reshape(1, LANES)
    return pl.pallas_call(
        _attn_kernel,
        grid=(b, pairs),
        in_specs=[spec, spec, spec, spec, gspec, gspec],
        out_specs=spec,
        out_shape=jax.ShapeDtypeStruct((b, s, width), BF16),
        scratch_shapes=[
            pltpu.VMEM((s, LANES), F32), pltpu.VMEM((s, LANES), F32),
            pltpu.VMEM((5, s, LANES), BF16),
            pltpu.VMEM((5, s, LANES), BF16),
            pltpu.VMEM((3, s, LANES), F32),
            pltpu.VMEM((s, LANES), F32),
            pltpu.VMEM((QBLK, 2 * QBLK), F32), pltpu.VMEM((QBLK, QBLK), F32), pltpu.VMEM((QBLK, 4 * QBLK), F32),
        ],
        compiler_params=pltpu.CompilerParams(
            dimension_semantics=("arbitrary", "arbitrary"), vmem_limit_bytes=VMEM_LIMIT),
        name="attn",
    )(q, k, v, gate, tile2(q_gain), tile2(k_gain))


def _ssm_kernel(u_ref, wb_ref, wcr_ref, wci_ref, are_ref, aim_ref, d_ref, y_ref,
                ut, sre, sim, *, batch):
    nslab, nj = u_ref.shape[0], u_ref.shape[1]
    steps = nslab * TIME_TILE

    @pl.when(pl.program_id(0) == 0)
    def _():
        sre[...] = jnp.zeros_like(sre)
        sim[...] = jnp.zeros_like(sim)

    for c in range(nslab):
        for j in range(nj):
            src = u_ref.at[c, j]
            for t in range(TIME_TILE):
                ut[pl.ds((c * TIME_TILE + t) * batch, batch), j * LANES:(j + 1) * LANES] = (
                    src[pl.ds(t, batch, stride=TIME_TILE), :])

    sqrt_half = math.sqrt(0.5)

    def chunk(j):
        sl = slice(j * STATE_CHUNK, (j + 1) * STATE_CHUNK)
        lanes = slice(j * LANES, (j + 1) * LANES)
        uj = ut[:, lanes]
        bu = jnp.dot(uj.astype(BF16), wb_ref[j], preferred_element_type=F32)
        yield
        a_re = jnp.broadcast_to(are_ref[:, sl], (batch, STATE_CHUNK))
        a_im = jnp.broadcast_to(aim_ref[:, sl], (batch, STATE_CHUNK))
        x_re, x_im = sre[:, sl], sim[:, sl]
        hist_re, hist_im = [], []
        for t in range(steps):
            b_re = bu[t * batch:(t + 1) * batch, :STATE_CHUNK]
            b_im = bu[t * batch:(t + 1) * batch, STATE_CHUNK:]
            x_re, x_im = a_re * x_re - a_im * x_im + b_re, a_re * x_im + a_im * x_re + b_im
            hist_re.append(x_re.astype(BF16))
            hist_im.append(x_im.astype(BF16))
        sre[:, sl] = x_re
        sim[:, sl] = x_im
        yield
        y = (jnp.dot(jnp.concatenate(hist_re, axis=0), wcr_ref[j], preferred_element_type=F32)
             - jnp.dot(jnp.concatenate(hist_im, axis=0), wci_ref[j], preferred_element_type=F32))
        yield
        y = y + d_ref[:, lanes] * uj
        yg = 0.5 * y * (1.0 + lax.erf(y * sqrt_half))
        for c in range(nslab):
            dst = y_ref.at[c, j]
            for t in range(TIME_TILE):
                r0 = (c * TIME_TILE + t) * batch
                dst[pl.ds(t, batch, stride=TIME_TILE), :] = yg[r0:r0 + batch, :]

    _interleave([chunk(j) for j in range(nj)])


def _ssm(u_tiles, batch, w_b, w_cre, w_cim, a_re, a_im, d_skip):
    nslab_total, nj, _, _ = u_tiles.shape
    nslab = SSM_STEPS // TIME_TILE
    rows = SSM_STEPS * batch
    nstate = a_re.shape[1]
    tile_spec = pl.BlockSpec((nslab, nj, batch * TIME_TILE, LANES), lambda t: (t, 0, 0, 0))
    full = lambda a: pl.BlockSpec(a.shape, lambda t: (0,) * a.ndim)
    d2 = d_skip.reshape(1, -1)
    return pl.pallas_call(
        functools.partial(_ssm_kernel, batch=batch),
        grid=(nslab_total // nslab,),
        in_specs=[tile_spec, full(w_b), full(w_cre), full(w_cim), full(a_re), full(a_im), full(d2)],
        out_specs=tile_spec,
        out_shape=jax.ShapeDtypeStruct(u_tiles.shape, F32),
        scratch_shapes=[
            pltpu.VMEM((rows, nj * LANES), F32),
            pltpu.VMEM((batch, nstate), F32), pltpu.VMEM((batch, nstate), F32),
        ],
        compiler_params=pltpu.CompilerParams(
            dimension_semantics=("arbitrary",), vmem_limit_bytes=VMEM_LIMIT),
        name="ssm",
    )(u_tiles, w_b, w_cre, w_cim, a_re, a_im, d2)


def _out_kernel(attn_ref, y_ref, gs_ref, x_ref, p_ref, wglu_ref, bglu_ref, wout_ref, gple_ref,
                wg_ref, wp_ref, o_ref):
    nslab, nj = y_ref.shape[0], y_ref.shape[1]
    rows = OUT_CHUNK_ROWS
    aw = attn_ref.shape[-1]

    def chunk(c):
        r = pl.ds(c * rows, rows)
        slabs = slice(c * rows // TIME_TILE, (c + 1) * rows // TIME_TILE)
        yg = jnp.concatenate([y_ref[slabs, j].reshape(rows, LANES) for j in range(nj)], axis=1)
        glu = jnp.dot(yg.astype(BF16), wglu_ref[...], preferred_element_type=F32) + bglu_ref[...]
        pp = jnp.dot(p_ref[0, r, :].astype(BF16), wp_ref[...], preferred_element_type=F32)
        yield
        gs = gs_ref[0, r, :]
        ssm = yg * _sigmoid(glu) * (gs * _sigmoid(gs))
        h = (x_ref[0, r, :]
             + jnp.dot(attn_ref[0, r, :], wout_ref[:aw, :], preferred_element_type=F32)
             + jnp.dot(ssm.astype(BF16), wout_ref[aw:, :], preferred_element_type=F32))
        yield
        ms = jnp.mean(h * h, axis=-1, keepdims=True)
        hn = (h * lax.rsqrt(ms + EPS) * gple_ref[...]).astype(BF16)
        gate = _sigmoid(jnp.dot(hn, wg_ref[...], preferred_element_type=F32))
        yield
        o_ref[0, r, :] = h + gate * pp

    _interleave([chunk(c) for c in range(nslab * TIME_TILE // rows)])


def _out_stage(attn, y_tiles, gate_s, x, p, w_glu, b_glu, w_out, ple_gain, w_gate, w_proj):
    b, s, d = x.shape
    rows = OUT_ROW_TILE
    nj = y_tiles.shape[1]
    nat = lambda a: pl.BlockSpec((1, rows, a.shape[-1]), lambda i, t: (i, t, 0))
    full = lambda a: pl.BlockSpec(a.shape, lambda i, t: (0,) * a.ndim)
    y_spec = pl.BlockSpec((rows // TIME_TILE, nj, TIME_TILE, LANES), lambda i, t: (t, 0, i, 0))
    b2 = b_glu.reshape(1, -1)
    g2 = ple_gain.reshape(1, -1)
    return pl.pallas_call(
        _out_kernel,
        grid=(b, s // rows),
        in_specs=[nat(attn), y_spec, nat(gate_s), nat(x), nat(p), full(w_glu), full(b2), full(w_out),
                  full(g2), full(w_gate), full(w_proj)],
        out_specs=nat(x),
        out_shape=jax.ShapeDtypeStruct(x.shape, x.dtype),
        compiler_params=pltpu.CompilerParams(
            dimension_semantics=("arbitrary", "arbitrary"), vmem_limit_bytes=VMEM_LIMIT),
        name="outstage",
    )(attn, y_tiles, gate_s, x, p, w_glu, b2, w_out, g2, w_gate, w_proj)


def kernel(x, p, mix_norm, w_in, q_norm, k_norm, lambda_re, lambda_im, log_dt, b_re, b_im, c_re, c_im,
           d_skip, w_glu, b_glu, w_out, ple_norm, w_ple_gate, w_ple_proj):
    batch = x.shape[0]
    h = x
    for i in range(w_in.shape[0]):
        q, k, v, gate_a, u_tiles, gate_s = _inproj(h, mix_norm[i], w_in[i].astype(BF16))
        a_re, a_im, w_b, w_cre, w_cim = _ssm_params(
            lambda_re[i], lambda_im[i], log_dt[i], b_re[i], b_im[i], c_re[i], c_im[i])
        attn = _attention(q, k, v, gate_a, q_norm[i], k_norm[i])
        y_tiles = _ssm(u_tiles, batch, w_b, w_cre, w_cim, a_re, a_im, d_skip[i])
        h = _out_stage(attn, y_tiles, gate_s, h, p[i], w_glu[i].astype(BF16), b_glu[i],
                       w_out[i].astype(BF16), ple_norm[i], w_ple_gate[i].astype(BF16),
                       w_ple_proj[i].astype(BF16))
    return h
```

```python
import functools
import math

import jax
import jax.numpy as jnp
from jax import lax
from jax.experimental import pallas as pl
from jax.experimental.pallas import tpu as pltpu

F32 = jnp.float32
BF16 = jnp.bfloat16

HEAD_DIM = 64
SSM_GROUP = 16
SSM_STATE = 64
DILATED_CONFIGS = ((128, 1), (512, 4), (2048, 16))
EPS = 1e-6

LANES = 128
QBLK = 128
NEG = -1e30
LOG2E = math.log2(math.e)
IN_ROW_TILE = 1024
OUT_ROW_TILE = 1024
OUT_CHUNK_ROWS = 256
SSM_STEPS = 32
TIME_TILE = 8
GROUPS_PER_CHUNK = LANES // SSM_GROUP
STATE_CHUNK = GROUPS_PER_CHUNK * SSM_STATE
VMEM_LIMIT = 56 * 1024 * 1024


def _sigmoid(x):
    return jax.nn.sigmoid(x)


def _advance(gens):
    for g in list(gens):
        try:
            next(g)
        except StopIteration:
            gens.remove(g)


def _cast_once(pairs):
    first = functools.reduce(jnp.logical_and, [pl.program_id(a) == 0 for a in range(2)])

    @pl.when(first)
    def _():
        for src, dst in pairs:
            for c in range(0, src.shape[1], 4 * LANES):
                dst[:, c:c + 4 * LANES] = src[:, c:c + 4 * LANES].astype(dst.dtype)


def _interleave(tasks):
    pending, active = list(tasks), []
    while pending or active:
        if pending:
            active.append(pending.pop(0))
        _advance(active)


def _inproj_kernel(x_ref, g_ref, w32_ref, q_ref, k_ref, v_ref, ga_ref, u_ref, gs_ref, w_ref, *, width):
    _cast_once([(w32_ref, w_ref)])
    x = x_ref[0]
    ms = jnp.mean(x * x, axis=-1, keepdims=True)
    xn = (x * lax.rsqrt(ms + EPS) * g_ref[...]).astype(BF16)

    def proj(idx):
        return jnp.dot(xn, w_ref[:, idx * width:(idx + 1) * width], preferred_element_type=F32)

    q_ref[0] = proj(0)
    k_ref[0] = proj(1)
    v_ref[0] = proj(2)
    ga_ref[0] = proj(3)
    u = proj(4)
    for c in range(u.shape[0] // TIME_TILE):
        for j in range(width // LANES):
            u_ref[c, j] = u[c * TIME_TILE:(c + 1) * TIME_TILE, j * LANES:(j + 1) * LANES]
    gs_ref[0] = proj(5)


def _inproj(x, gain, w):
    b, s, d = x.shape
    width = w.shape[1] // 6
    nj = width // LANES
    rows = IN_ROW_TILE
    nat = jax.ShapeDtypeStruct((b, s, width), F32)
    nat_spec = pl.BlockSpec((1, rows, width), lambda i, t: (i, t, 0))
    u_shape = jax.ShapeDtypeStruct((s // TIME_TILE, nj, b * TIME_TILE, LANES), F32)
    u_spec = pl.BlockSpec((rows // TIME_TILE, nj, TIME_TILE, LANES), lambda i, t: (t, 0, i, 0))
    return pl.pallas_call(
        functools.partial(_inproj_kernel, width=width),
        grid=(b, s // rows),
        in_specs=[
            pl.BlockSpec((1, rows, d), lambda i, t: (i, t, 0)),
            pl.BlockSpec((1, d), lambda i, t: (0, 0)),
            pl.BlockSpec(w.shape, lambda i, t: (0, 0), pipeline_mode=pl.Buffered(1)),
        ],
        out_specs=[nat_spec, nat_spec, nat_spec, nat_spec, u_spec, nat_spec],
        out_shape=[nat, nat, nat, nat, u_shape, nat],
        scratch_shapes=[pltpu.VMEM(w.shape, BF16)],
        compiler_params=pltpu.CompilerParams(
            dimension_semantics=("arbitrary", "arbitrary"), vmem_limit_bytes=VMEM_LIMIT),
        name="inproj",
    )(x, gain.reshape(1, d), w)


def _zoh(lam_re, lam_im, dt):
    mag = jnp.exp(lam_re * dt)
    a_re = mag * jnp.cos(lam_im * dt)
    a_im = mag * jnp.sin(lam_im * dt)
    den = lam_re * lam_re + lam_im * lam_im
    num_re = a_re - 1.0
    coef_re = (num_re * lam_re + a_im * lam_im) / den
    coef_im = (a_im * lam_re - num_re * lam_im) / den
    return a_re, a_im, coef_re, coef_im


def _zoh_kernel(lre_ref, lim_ref, ldt_ref, lre_rep_ref, lim_rep_ref, bre_ref, bim_ref,
                are_ref, aim_ref, bbre_ref, bbim_ref):
    dt = jnp.exp(ldt_ref[...])
    a_re, a_im, _, _ = _zoh(lre_ref[...], lim_ref[...], dt)
    are_ref[...] = a_re
    aim_ref[...] = a_im
    _, _, coef_re, coef_im = _zoh(lre_rep_ref[...], lim_rep_ref[...], dt)
    b_re = bre_ref[...]
    b_im = bim_ref[...]
    bbre_ref[...] = coef_re * b_re - coef_im * b_im
    bbim_ref[...] = coef_re * b_im + coef_im * b_re


def _ssm_params(lam_re, lam_im, log_dt, b_re, b_im, c_re, c_im):
    g, n = lam_re.shape
    c = b_re.shape[-1]
    rep = lambda a: jnp.repeat(a, c, axis=1)
    small = jax.ShapeDtypeStruct((g, n), F32)
    wide = jax.ShapeDtypeStruct((g, n * c), F32)
    a_re, a_im, bb_re, bb_im = pl.pallas_call(
        _zoh_kernel, out_shape=[small, small, wide, wide], name="zoh",
    )(lam_re, lam_im, log_dt.reshape(g, 1), rep(lam_re), rep(lam_im),
      b_re.reshape(g, n * c), b_im.reshape(g, n * c))

    nch = g // GROUPS_PER_CHUNK
    eye = jnp.eye(GROUPS_PER_CHUNK, dtype=bool)

    def in_blockdiag(bb):
        t = bb.reshape(nch, GROUPS_PER_CHUNK, n, c).transpose(0, 1, 3, 2)
        m = jnp.where(eye[None, :, None, :, None], t[:, :, :, None, :], 0.0)
        return m.reshape(nch, GROUPS_PER_CHUNK * c, GROUPS_PER_CHUNK * n)

    def out_blockdiag(cc):
        t = cc.reshape(nch, GROUPS_PER_CHUNK, c, n).transpose(0, 1, 3, 2)
        m = jnp.where(eye[None, :, None, :, None], t[:, :, :, None, :], 0.0)
        return m.reshape(nch, GROUPS_PER_CHUNK * n, GROUPS_PER_CHUNK * c)

    w_b = jnp.concatenate([in_blockdiag(bb_re), in_blockdiag(bb_im)], axis=-1).astype(BF16)
    return (a_re.reshape(1, g * n), a_im.reshape(1, g * n), w_b,
            out_blockdiag(c_re).astype(BF16), out_blockdiag(c_im).astype(BF16))


def _attn_kernel(q_ref, k_ref, v_ref, ga_ref, gq_ref, gk_ref, o_ref,
                 qn, kn, ops_a, ops_b, res, out32, bias_full, bias_first, bias_mod4):
    seq = qn.shape[0]
    nblk = seq // QBLK
    per_class = nblk // 4
    assert DILATED_CONFIGS == ((128, 1), (512, 4), (2048, 16)) and nblk == 16
    lane = lax.broadcasted_iota(jnp.int32, (1, LANES), 1)
    head0 = lane < HEAD_DIM
    rr = lax.broadcasted_iota(jnp.int32, (2 * LANES, 2 * LANES), 0) // HEAD_DIM
    cc = lax.broadcasted_iota(jnp.int32, (2 * LANES, 2 * LANES), 1) // HEAD_DIM
    head_mean = jnp.where(rr == cc, 1.0 / HEAD_DIM, 0.0).astype(BF16)
    gq = gq_ref[...] * (HEAD_DIM ** -0.5 * LOG2E)
    gk = gk_ref[...]
    one0 = jnp.where(head0, 1.0, 0.0).astype(BF16)
    one1 = jnp.where(head0, 0.0, 1.0).astype(BF16)

    def blk(i, n=QBLK):
        return pl.ds(i * QBLK, n)

    def write_operands(ops, b, qv, kv, vv):
        rows = blk(b)
        vb = vv.astype(BF16)
        kb = kv.astype(BF16)
        zero = jnp.zeros_like(vb)
        ops[0, rows, :] = qv.astype(BF16)
        ops[1, rows, :] = jnp.where(head0, vb, zero)
        ops[2, rows, :] = jnp.where(head0, zero, vb)
        ops[3, rows, :] = jnp.where(head0, kb, zero)
        ops[4, rows, :] = jnp.where(head0, zero, kb)

    def norm_block(b):
        rows = blk(b)
        q = q_ref[0, rows, :]
        k = k_ref[0, rows, :]
        ms = jnp.dot(jnp.concatenate([(q * q).astype(BF16), (k * k).astype(BF16)], axis=1), head_mean,
                     preferred_element_type=F32)
        qv = q * lax.rsqrt(ms[:, :LANES] + EPS) * gq
        kv = k * lax.rsqrt(ms[:, LANES:] + EPS) * gk
        qn[rows, :] = qv
        kn[rows, :] = kv
        write_operands(ops_a, b, qv, kv, v_ref[0, rows, :])

    def to_mod4(c):
        src = pl.ds((c // per_class) + (c % per_class) * (4 * QBLK), QBLK, stride=4)
        write_operands(ops_b, c, qn[src, :], kn[src, :], v_ref.at[0][src, :])

    @pl.when((pl.program_id(0) == 0) & (pl.program_id(1) == 0))
    def _():
        qi = lax.broadcasted_iota(jnp.int32, (QBLK, 2 * QBLK), 0)
        kj = lax.broadcasted_iota(jnp.int32, (QBLK, 2 * QBLK), 1)
        bias_full[...] = jnp.where((kj >= qi) & (kj <= qi + QBLK), 0.0, NEG)
        qi1 = lax.broadcasted_iota(jnp.int32, (QBLK, QBLK), 0)
        kj1 = lax.broadcasted_iota(jnp.int32, (QBLK, QBLK), 1)
        bias_first[...] = jnp.where(kj1 <= qi1, 0.0, NEG)
        same4 = ((qi1 - kj1) & 3) == 0
        far = jnp.where(same4, 0.0, NEG)
        prev = jnp.where(same4, jnp.where(kj1 >= qi1, 1.0, 0.0), jnp.where(kj1 >= qi1, 0.0, NEG))
        cur = jnp.where(kj1 <= qi1, jnp.where(same4, 1.0, 0.0), NEG)
        for t, tile in enumerate((far, far, prev, cur)):
            bias_mod4[:, t * QBLK:(t + 1) * QBLK] = tile

    def unit(ops, b, k0, nk, bias, out):
        krows = pl.ds(k0, nk)
        kcat = jnp.concatenate([ops[3, krows, :], ops[4, krows, :]], axis=0)
        s2 = lax.dot_general(ops[0, blk(b), :], kcat, (((1,), (1,)), ((), ())),
                             preferred_element_type=F32)
        yield
        ss, ms = [], []
        for h in range(2):
            s = s2[:, h * nk:(h + 1) * nk] + bias
            ms.append(jnp.max(s, axis=-1, keepdims=True))
            ss.append(s)
        yield
        es = [jnp.exp2(s - m).astype(BF16) for s, m in zip(ss, ms)]
        yield
        rhs = jnp.concatenate([
            jnp.concatenate([ops[1, krows, :], jnp.broadcast_to(one0, (nk, LANES))], axis=1),
            jnp.concatenate([ops[2, krows, :], jnp.broadcast_to(one1, (nk, LANES))], axis=1)], axis=0)
        o = jnp.dot(jnp.concatenate(es, axis=1), rhs, preferred_element_type=F32)
        yield
        out.extend([jnp.where(head0, ms[0], ms[1]), o[:, :LANES], o[:, LANES:]])

    def unit_natural(b):
        part = []
        if b == 0:
            yield from unit(ops_a, b, 0, QBLK, bias_first[...], part)
        else:
            yield from unit(ops_a, b, (b - 1) * QBLK, 2 * QBLK, bias_full[...], part)
        for k in range(3):
            res[k, blk(b), :] = part[k]
        yield
        to_mod4(b)

    def unit_mod4(b):
        r, i = b // per_class, b % per_class
        rows = pl.ds(r + i * (4 * QBLK), QBLK, stride=4)
        seg = r * per_class * QBLK
        near = min(i + 1, 2)
        part_near, part_far = [], []
        gens = [unit(ops_b, b, seg + (i + 1 - near) * QBLK, near * QBLK,
                     bias_mod4[:, (per_class - near) * QBLK:], part_near)]
        if i >= 2:
            gens.append(unit(ops_b, b, seg, (i - 1) * QBLK, bias_mod4[:, :(i - 1) * QBLK], part_far))
        while gens:
            _advance(gens)
            yield
        parts = [[res.at[k][rows, :] for k in range(3)], part_near] + ([part_far] if part_far else [])
        mx = functools.reduce(jnp.maximum, [p[0] for p in parts])
        ws = [jnp.exp2(p[0] - mx) for p in parts]
        num = functools.reduce(jnp.add, [w * p[1] for w, p in zip(ws, parts)])
        den = functools.reduce(jnp.add, [w * p[2] for w, p in zip(ws, parts)])
        g = ga_ref.at[0][rows, :]
        out32[rows, :] = num / den * (g * _sigmoid(g))

    def cast_out(i):
        o_ref[0, blk(i), :] = out32[blk(i), :].astype(o_ref.dtype)

    for b in range(nblk):
        norm_block(b)
    _interleave([unit_natural(b) for b in range(nblk)])
    _interleave([unit_mod4(b) for b in range(nblk)])
    for i in range(nblk):
        cast_out(i)


def _attention(q, k, v, gate, q_gain, k_gain):
    b, s, width = q.shape
    pairs = width // LANES
    spec = pl.BlockSpec((1, s, LANES), lambda i, h: (i, 0, h))
    gspec = pl.BlockSpec((1, LANES), lambda i, h: (0, 0))
    tile2 = lambda g: jnp.concatenate([g, g]).reshape(1, LANES)
    return pl.pallas_call(
        _attn_kernel,
        grid=(b, pairs),
        in_specs=[spec, spec, spec, spec, gspec, gspec],
        out_specs=spec,
        out_shape=jax.ShapeDtypeStruct((b, s, width), BF16),
        scratch_shapes=[
            pltpu.VMEM((s, LANES), F32), pltpu.VMEM((s, LANES), F32),
            pltpu.VMEM((5, s, LANES), BF16),
            pltpu.VMEM((5, s, LANES), BF16),
            pltpu.VMEM((3, s, LANES), F32),
            pltpu.VMEM((s, LANES), F32),
            pltpu.VMEM((QBLK, 2 * QBLK), F32), pltpu.VMEM((QBLK, QBLK), F32), pltpu.VMEM((QBLK, 4 * QBLK), F32),
        ],
        compiler_params=pltpu.CompilerParams(
            dimension_semantics=("arbitrary", "arbitrary"), vmem_limit_bytes=VMEM_LIMIT),
        name="attn",
    )(q, k, v, gate, tile2(q_gain), tile2(k_gain))


def _ssm_kernel(u_ref, wb_ref, wcr_ref, wci_ref, are_ref, aim_ref, d_ref, y_ref,
                ut, sre, sim, *, batch):
    nslab, nj = u_ref.shape[0], u_ref.shape[1]
    steps = nslab * TIME_TILE

    @pl.when(pl.program_id(0) == 0)
    def _():
        sre[...] = jnp.zeros_like(sre)
        sim[...] = jnp.zeros_like(sim)

    for c in range(nslab):
        for j in range(nj):
            src = u_ref.at[c, j]
            for t in range(TIME_TILE):
                ut[pl.ds((c * TIME_TILE + t) * batch, batch), j * LANES:(j + 1) * LANES] = (
                    src[pl.ds(t, batch, stride=TIME_TILE), :])

    sqrt_half = math.sqrt(0.5)

    def chunk(j):
        sl = slice(j * STATE_CHUNK, (j + 1) * STATE_CHUNK)
        lanes = slice(j * LANES, (j + 1) * LANES)
        uj = ut[:, lanes]
        bu = jnp.dot(uj.astype(BF16), wb_ref[j], preferred_element_type=F32)
        yield
        a_re = jnp.broadcast_to(are_ref[:, sl], (batch, STATE_CHUNK))
        a_im = jnp.broadcast_to(aim_ref[:, sl], (batch, STATE_CHUNK))
        x_re, x_im = sre[:, sl], sim[:, sl]
        hist_re, hist_im = [], []
        for t in range(steps):
            b_re = bu[t * batch:(t + 1) * batch, :STATE_CHUNK]
            b_im = bu[t * batch:(t + 1) * batch, STATE_CHUNK:]
            x_re, x_im = a_re * x_re - a_im * x_im + b_re, a_re * x_im + a_im * x_re + b_im
            hist_re.append(x_re.astype(BF16))
            hist_im.append(x_im.astype(BF16))
        sre[:, sl] = x_re
        sim[:, sl] = x_im
        yield
        y = (jnp.dot(jnp.concatenate(hist_re, axis=0), wcr_ref[j], preferred_element_type=F32)
             - jnp.dot(jnp.concatenate(hist_im, axis=0), wci_ref[j], preferred_element_type=F32))
        yield
        y = y + d_ref[:, lanes] * uj
        yg = 0.5 * y * (1.0 + lax.erf(y * sqrt_half))
        for c in range(nslab):
            dst = y_ref.at[c, j]
            for t in range(TIME_TILE):
                r0 = (c * TIME_TILE + t) * batch
                dst[pl.ds(t, batch, stride=TIME_TILE), :] = yg[r0:r0 + batch, :]

    _interleave([chunk(j) for j in range(nj)])


def _ssm(u_tiles, batch, w_b, w_cre, w_cim, a_re, a_im, d_skip):
    nslab_total, nj, _, _ = u_tiles.shape
    nslab = SSM_STEPS // TIME_TILE
    rows = SSM_STEPS * batch
    nstate = a_re.shape[1]
    tile_spec = pl.BlockSpec((nslab, nj, batch * TIME_TILE, LANES), lambda t: (t, 0, 0, 0))
    full = lambda a: pl.BlockSpec(a.shape, lambda t: (0,) * a.ndim)
    d2 = d_skip.reshape(1, -1)
    return pl.pallas_call(
        functools.partial(_ssm_kernel, batch=batch),
        grid=(nslab_total // nslab,),
        in_specs=[tile_spec, full(w_b), full(w_cre), full(w_cim), full(a_re), full(a_im), full(d2)],
        out_specs=tile_spec,
        out_shape=jax.ShapeDtypeStruct(u_tiles.shape, F32),
        scratch_shapes=[
            pltpu.VMEM((rows, nj * LANES), F32),
            pltpu.VMEM((batch, nstate), F32), pltpu.VMEM((batch, nstate), F32),
        ],
        compiler_params=pltpu.CompilerParams(
            dimension_semantics=("arbitrary",), vmem_limit_bytes=VMEM_LIMIT),
        name="ssm",
    )(u_tiles, w_b, w_cre, w_cim, a_re, a_im, d2)


def _out_kernel(attn_ref, y_ref, gs_ref, x_ref, p_ref, wglu32_ref, bglu_ref, wout32_ref, gple_ref,
                wg32_ref, wp32_ref, o_ref, wglu_ref, wout_ref, wg_ref, wp_ref):
    _cast_once([(wglu32_ref, wglu_ref), (wout32_ref, wout_ref), (wg32_ref, wg_ref), (wp32_ref, wp_ref)])
    nslab, nj = y_ref.shape[0], y_ref.shape[1]
    rows = OUT_CHUNK_ROWS
    aw = attn_ref.shape[-1]

    def chunk(c):
        r = pl.ds(c * rows, rows)
        slabs = slice(c * rows // TIME_TILE, (c + 1) * rows // TIME_TILE)
        yg = jnp.concatenate([y_ref[slabs, j].reshape(rows, LANES) for j in range(nj)], axis=1)
        glu = jnp.dot(yg.astype(BF16), wglu_ref[...], preferred_element_type=F32) + bglu_ref[...]
        pp = jnp.dot(p_ref[0, r, :].astype(BF16), wp_ref[...], preferred_element_type=F32)
        yield
        gs = gs_ref[0, r, :]
        ssm = yg * _sigmoid(glu) * (gs * _sigmoid(gs))
        h = (x_ref[0, r, :]
             + jnp.dot(attn_ref[0, r, :], wout_ref[:aw, :], preferred_element_type=F32)
             + jnp.dot(ssm.astype(BF16), wout_ref[aw:, :], preferred_element_type=F32))
        yield
        ms = jnp.mean(h * h, axis=-1, keepdims=True)
        hn = (h * lax.rsqrt(ms + EPS) * gple_ref[...]).astype(BF16)
        gate = _sigmoid(jnp.dot(hn, wg_ref[...], preferred_element_type=F32))
        yield
        o_ref[0, r, :] = h + gate * pp

    _interleave([chunk(c) for c in range(nslab * TIME_TILE // rows)])


def _out_stage(attn, y_tiles, gate_s, x, p, w_glu, b_glu, w_out, ple_gain, w_gate, w_proj):
    b, s, d = x.shape
    rows = OUT_ROW_TILE
    nj = y_tiles.shape[1]
    nat = lambda a: pl.BlockSpec((1, rows, a.shape[-1]), lambda i, t: (i, t, 0))
    full = lambda a: pl.BlockSpec(a.shape, lambda i, t: (0,) * a.ndim, pipeline_mode=pl.Buffered(1))
    y_spec = pl.BlockSpec((rows // TIME_TILE, nj, TIME_TILE, LANES), lambda i, t: (t, 0, i, 0))
    b2 = b_glu.reshape(1, -1)
    g2 = ple_gain.reshape(1, -1)
    return pl.pallas_call(
        _out_kernel,
        grid=(b, s // rows),
        in_specs=[nat(attn), y_spec, nat(gate_s), nat(x), nat(p), full(w_glu), full(b2), full(w_out),
                  full(g2), full(w_gate), full(w_proj)],
        out_specs=nat(x),
        out_shape=jax.ShapeDtypeStruct(x.shape, x.dtype),
        scratch_shapes=[pltpu.VMEM(w.shape, BF16) for w in (w_glu, w_out, w_gate, w_proj)],
        compiler_params=pltpu.CompilerParams(
            dimension_semantics=("arbitrary", "arbitrary"), vmem_limit_bytes=VMEM_LIMIT),
        name="outstage",
    )(attn, y_tiles, gate_s, x, p, w_glu, b2, w_out, g2, w_gate, w_proj)


def kernel(x, p, mix_norm, w_in, q_norm, k_norm, lambda_re, lambda_im, log_dt, b_re, b_im, c_re, c_im,
           d_skip, w_glu, b_glu, w_out, ple_norm, w_ple_gate, w_ple_proj):
    batch = x.shape[0]
    h = x
    for i in range(w_in.shape[0]):
        q, k, v, gate_a, u_tiles, gate_s = _inproj(h, mix_norm[i], w_in[i])
        a_re, a_im, w_b, w_cre, w_cim = _ssm_params(
            lambda_re[i], lambda_im[i], log_dt[i], b_re[i], b_im[i], c_re[i], c_im[i])
        attn = _attention(q, k, v, gate_a, q_norm[i], k_norm[i])
        y_tiles = _ssm(u_tiles, batch, w_b, w_cre, w_cim, a_re, a_im, d_skip[i])
        h = _out_stage(attn, y_tiles, gate_s, h, p[i], w_glu[i], b_glu[i], w_out[i], ple_norm[i],
                       w_ple_gate[i], w_ple_proj[i])
    return h
```

```python
import functools
import math

import jax
import jax.numpy as jnp
from jax import lax
from jax.experimental import pallas as pl
from jax.experimental.pallas import tpu as pltpu

F32 = jnp.float32
BF16 = jnp.bfloat16

HEAD_DIM = 64
SSM_GROUP = 16
SSM_STATE = 64
DILATED_CONFIGS = ((128, 1), (512, 4), (2048, 16))
EPS = 1e-6

LANES = 128
QBLK = 128
NEG = -1e30
LOG2E = math.log2(math.e)
IN_ROW_TILE = 1024
OUT_ROW_TILE = 1024
OUT_CHUNK_ROWS = 256
SSM_STEPS = 32
TIME_TILE = 8
GROUPS_PER_CHUNK = LANES // SSM_GROUP
STATE_CHUNK = GROUPS_PER_CHUNK * SSM_STATE
VMEM_LIMIT = 56 * 1024 * 1024


def _sigmoid(x):
    return jax.nn.sigmoid(x)


def _advance(gens):
    for g in list(gens):
        try:
            next(g)
        except StopIteration:
            gens.remove(g)


def _cast_once(pairs):
    first = functools.reduce(jnp.logical_and, [pl.program_id(a) == 0 for a in range(2)])

    @pl.when(first)
    def _():
        for src, dst in pairs:
            for c in range(0, src.shape[1], 4 * LANES):
                dst[:, c:c + 4 * LANES] = src[:, c:c + 4 * LANES].astype(dst.dtype)


def _interleave(tasks):
    pending, active = list(tasks), []
    while pending or active:
        if pending:
            active.append(pending.pop(0))
        _advance(active)


def _inproj_kernel(x_ref, g_ref, w32_ref, q_ref, k_ref, v_ref, ga_ref, u_ref, gs_ref, w_ref, *, width):
    _cast_once([(w32_ref, w_ref)])
    x = x_ref[0]
    ms = jnp.mean(x * x, axis=-1, keepdims=True)
    xn = (x * lax.rsqrt(ms + EPS) * g_ref[...]).astype(BF16)

    def proj(idx):
        return jnp.dot(xn, w_ref[:, idx * width:(idx + 1) * width], preferred_element_type=F32)

    q_ref[0] = proj(0)
    k_ref[0] = proj(1)
    v_ref[0] = proj(2)
    ga_ref[0] = proj(3)
    u = proj(4)
    for c in range(u.shape[0] // TIME_TILE):
        for j in range(width // LANES):
            u_ref[c, j] = u[c * TIME_TILE:(c + 1) * TIME_TILE, j * LANES:(j + 1) * LANES]
    gs_ref[0] = proj(5)


def _inproj(x, gain, w):
    b, s, d = x.shape
    width = w.shape[1] // 6
    nj = width // LANES
    rows = IN_ROW_TILE
    nat = jax.ShapeDtypeStruct((b, s, width), F32)
    nat_spec = pl.BlockSpec((1, rows, width), lambda i, t: (i, t, 0))
    u_shape = jax.ShapeDtypeStruct((s // TIME_TILE, nj, b * TIME_TILE, LANES), F32)
    u_spec = pl.BlockSpec((rows // TIME_TILE, nj, TIME_TILE, LANES), lambda i, t: (t, 0, i, 0))
    return pl.pallas_call(
        functools.partial(_inproj_kernel, width=width),
        grid=(b, s // rows),
        in_specs=[
            pl.BlockSpec((1, rows, d), lambda i, t: (i, t, 0)),
            pl.BlockSpec((1, d), lambda i, t: (0, 0)),
            pl.BlockSpec(w.shape, lambda i, t: (0, 0), pipeline_mode=pl.Buffered(1)),
        ],
        out_specs=[nat_spec, nat_spec, nat_spec, nat_spec, u_spec, nat_spec],
        out_shape=[nat, nat, nat, nat, u_shape, nat],
        scratch_shapes=[pltpu.VMEM(w.shape, BF16)],
        compiler_params=pltpu.CompilerParams(
            dimension_semantics=("arbitrary", "arbitrary"), vmem_limit_bytes=VMEM_LIMIT),
        name="inproj",
    )(x, gain, w)


def _zoh(lam_re, lam_im, dt):
    mag = jnp.exp(lam_re * dt)
    a_re = mag * jnp.cos(lam_im * dt)
    a_im = mag * jnp.sin(lam_im * dt)
    den = lam_re * lam_re + lam_im * lam_im
    num_re = a_re - 1.0
    coef_re = (num_re * lam_re + a_im * lam_im) / den
    coef_im = (a_im * lam_re - num_re * lam_im) / den
    return a_re, a_im, coef_re, coef_im


def _zoh_kernel(lre_ref, lim_ref, ldt_ref, bre_ref, bim_ref, are_ref, aim_ref, bbre_ref, bbim_ref):
    g, n = lre_ref.shape
    c = bre_ref.shape[0] // g
    dt = jnp.exp(ldt_ref[...])
    a_re, a_im, coef_re, coef_im = _zoh(lre_ref[...], lim_ref[...], dt)
    are_ref[...] = a_re
    aim_ref[...] = a_im
    rows = lambda a: jnp.broadcast_to(a[:, None, :], (g, c, n)).reshape(g * c, n)
    coef_re, coef_im = rows(coef_re), rows(coef_im)
    b_re = bre_ref[...]
    b_im = bim_ref[...]
    bbre_ref[...] = coef_re * b_re - coef_im * b_im
    bbim_ref[...] = coef_re * b_im + coef_im * b_re


def _ssm_params(lam_re, lam_im, log_dt, b_re, b_im, c_re, c_im):
    g, n = lam_re.shape
    c = b_re.shape[-1]
    rows_gc = lambda b: b.transpose(0, 2, 1).reshape(g * c, n)
    small = jax.ShapeDtypeStruct((g, n), F32)
    tall = jax.ShapeDtypeStruct((g * c, n), F32)
    a_re, a_im, bb_re, bb_im = pl.pallas_call(
        _zoh_kernel, out_shape=[small, small, tall, tall], name="zoh",
    )(lam_re, lam_im, log_dt.reshape(g, 1), rows_gc(b_re), rows_gc(b_im))

    nch = g // GROUPS_PER_CHUNK

    def blockdiag(m, rows_per_group, cols_per_group):
        t = jnp.tile(m.reshape(nch, GROUPS_PER_CHUNK * rows_per_group, cols_per_group), (1, 1, GROUPS_PER_CHUNK))
        rg = lax.broadcasted_iota(jnp.int32, t.shape, 1) // rows_per_group
        cg = lax.broadcasted_iota(jnp.int32, t.shape, 2) // cols_per_group
        return jnp.where(rg == cg, t, 0.0)

    w_b = jnp.concatenate([blockdiag(bb_re, c, n), blockdiag(bb_im, c, n)], axis=-1).astype(BF16)
    rows_gn = lambda cc: cc.transpose(0, 2, 1).reshape(g * n, c)
    return (a_re.reshape(1, g * n), a_im.reshape(1, g * n), w_b,
            blockdiag(rows_gn(c_re), n, c).astype(BF16), blockdiag(rows_gn(c_im), n, c).astype(BF16))


def _attn_kernel(q_ref, k_ref, v_ref, ga_ref, gq_ref, gk_ref, o_ref,
                 qn, kn, ops_a, ops_b, res, out32, bias_full, bias_first, bias_mod4):
    seq = qn.shape[0]
    nblk = seq // QBLK
    per_class = nblk // 4
    assert DILATED_CONFIGS == ((128, 1), (512, 4), (2048, 16)) and nblk == 16
    lane = lax.broadcasted_iota(jnp.int32, (1, LANES), 1)
    head0 = lane < HEAD_DIM
    rr = lax.broadcasted_iota(jnp.int32, (2 * LANES, 2 * LANES), 0) // HEAD_DIM
    cc = lax.broadcasted_iota(jnp.int32, (2 * LANES, 2 * LANES), 1) // HEAD_DIM
    head_mean = jnp.where(rr == cc, 1.0 / HEAD_DIM, 0.0).astype(BF16)
    gq = gq_ref[...] * (HEAD_DIM ** -0.5 * LOG2E)
    gk = gk_ref[...]
    one0 = jnp.where(head0, 1.0, 0.0).astype(BF16)
    one1 = jnp.where(head0, 0.0, 1.0).astype(BF16)

    def blk(i, n=QBLK):
        return pl.ds(i * QBLK, n)

    def write_operands(ops, b, qv, kv, vv):
        rows = blk(b)
        vb = vv.astype(BF16)
        kb = kv.astype(BF16)
        zero = jnp.zeros_like(vb)
        ops[0, rows, :] = qv.astype(BF16)
        ops[1, rows, :] = jnp.where(head0, vb, zero)
        ops[2, rows, :] = jnp.where(head0, zero, vb)
        ops[3, rows, :] = jnp.where(head0, kb, zero)
        ops[4, rows, :] = jnp.where(head0, zero, kb)

    def norm_block(b):
        rows = blk(b)
        q = q_ref[0, rows, :]
        k = k_ref[0, rows, :]
        ms = jnp.dot(jnp.concatenate([(q * q).astype(BF16), (k * k).astype(BF16)], axis=1), head_mean,
                     preferred_element_type=F32)
        qv = q * lax.rsqrt(ms[:, :LANES] + EPS) * gq
        kv = k * lax.rsqrt(ms[:, LANES:] + EPS) * gk
        qn[rows, :] = qv
        kn[rows, :] = kv
        write_operands(ops_a, b, qv, kv, v_ref[0, rows, :])

    def to_mod4(c):
        src = pl.ds((c // per_class) + (c % per_class) * (4 * QBLK), QBLK, stride=4)
        write_operands(ops_b, c, qn[src, :], kn[src, :], v_ref.at[0][src, :])

    @pl.when((pl.program_id(0) == 0) & (pl.program_id(1) == 0))
    def _():
        qi = lax.broadcasted_iota(jnp.int32, (QBLK, 2 * QBLK), 0)
        kj = lax.broadcasted_iota(jnp.int32, (QBLK, 2 * QBLK), 1)
        bias_full[...] = jnp.where((kj >= qi) & (kj <= qi + QBLK), 0.0, NEG)
        qi1 = lax.broadcasted_iota(jnp.int32, (QBLK, QBLK), 0)
        kj1 = lax.broadcasted_iota(jnp.int32, (QBLK, QBLK), 1)
        bias_first[...] = jnp.where(kj1 <= qi1, 0.0, NEG)
        same4 = ((qi1 - kj1) & 3) == 0
        far = jnp.where(same4, 0.0, NEG)
        prev = jnp.where(same4, jnp.where(kj1 >= qi1, 1.0, 0.0), jnp.where(kj1 >= qi1, 0.0, NEG))
        cur = jnp.where(kj1 <= qi1, jnp.where(same4, 1.0, 0.0), NEG)
        for t, tile in enumerate((far, far, prev, cur)):
            bias_mod4[:, t * QBLK:(t + 1) * QBLK] = tile

    def unit(ops, b, k0, nk, bias, out):
        krows = pl.ds(k0, nk)
        kcat = jnp.concatenate([ops[3, krows, :], ops[4, krows, :]], axis=0)
        s2 = lax.dot_general(ops[0, blk(b), :], kcat, (((1,), (1,)), ((), ())),
                             preferred_element_type=F32)
        yield
        ss, ms = [], []
        for h in range(2):
            s = s2[:, h * nk:(h + 1) * nk] + bias
            ms.append(jnp.max(s, axis=-1, keepdims=True))
            ss.append(s)
        yield
        es = [jnp.exp2(s - m).astype(BF16) for s, m in zip(ss, ms)]
        yield
        rhs = jnp.concatenate([
            jnp.concatenate([ops[1, krows, :], jnp.broadcast_to(one0, (nk, LANES))], axis=1),
            jnp.concatenate([ops[2, krows, :], jnp.broadcast_to(one1, (nk, LANES))], axis=1)], axis=0)
        o = jnp.dot(jnp.concatenate(es, axis=1), rhs, preferred_element_type=F32)
        yield
        out.extend([jnp.where(head0, ms[0], ms[1]), o[:, :LANES], o[:, LANES:]])

    def unit_natural(b):
        part = []
        if b == 0:
            yield from unit(ops_a, b, 0, QBLK, bias_first[...], part)
        else:
            yield from unit(ops_a, b, (b - 1) * QBLK, 2 * QBLK, bias_full[...], part)
        for k in range(3):
            res[k, blk(b), :] = part[k]
        yield
        to_mod4(b)

    def unit_mod4(b):
        r, i = b // per_class, b % per_class
        rows = pl.ds(r + i * (4 * QBLK), QBLK, stride=4)
        seg = r * per_class * QBLK
        near = min(i + 1, 2)
        part_near, part_far = [], []
        gens = [unit(ops_b, b, seg + (i + 1 - near) * QBLK, near * QBLK,
                     bias_mod4[:, (per_class - near) * QBLK:], part_near)]
        if i >= 2:
            gens.append(unit(ops_b, b, seg, (i - 1) * QBLK, bias_mod4[:, :(i - 1) * QBLK], part_far))
        while gens:
            _advance(gens)
            yield
        parts = [[res.at[k][rows, :] for k in range(3)], part_near] + ([part_far] if part_far else [])
        mx = functools.reduce(jnp.maximum, [p[0] for p in parts])
        ws = [jnp.exp2(p[0] - mx) for p in parts]
        num = functools.reduce(jnp.add, [w * p[1] for w, p in zip(ws, parts)])
        den = functools.reduce(jnp.add, [w * p[2] for w, p in zip(ws, parts)])
        g = ga_ref.at[0][rows, :]
        out32[rows, :] = num / den * (g * _sigmoid(g))

    def cast_out(i):
        o_ref[0, blk(i), :] = out32[blk(i), :].astype(o_ref.dtype)

    for b in range(nblk):
        norm_block(b)
    _interleave([unit_natural(b) for b in range(nblk)])
    _interleave([unit_mod4(b) for b in range(nblk)])
    for i in range(nblk):
        cast_out(i)


def _attention(q, k, v, gate, q_gain, k_gain):
    b, s, width = q.shape
    pairs = width // LANES
    spec = pl.BlockSpec((1, s, LANES), lambda i, h: (i, 0, h))
    gspec = pl.BlockSpec((1, LANES), lambda i, h: (0, 0))
    tile2 = lambda g: jnp.concatenate([g, g]).reshape(1, LANES)
    return pl.pallas_call(
        _attn_kernel,
        grid=(b, pairs),
        in_specs=[spec, spec, spec, spec, gspec, gspec],
        out_specs=spec,
        out_shape=jax.ShapeDtypeStruct((b, s, width), BF16),
        scratch_shapes=[
            pltpu.VMEM((s, LANES), F32), pltpu.VMEM((s, LANES), F32),
            pltpu.VMEM((5, s, LANES), BF16),
            pltpu.VMEM((5, s, LANES), BF16),
            pltpu.VMEM((3, s, LANES), F32),
            pltpu.VMEM((s, LANES), F32),
            pltpu.VMEM((QBLK, 2 * QBLK), F32), pltpu.VMEM((QBLK, QBLK), F32), pltpu.VMEM((QBLK, 4 * QBLK), F32),
        ],
        compiler_params=pltpu.CompilerParams(
            dimension_semantics=("arbitrary", "arbitrary"), vmem_limit_bytes=VMEM_LIMIT),
        name="attn",
    )(q, k, v, gate, tile2(q_gain), tile2(k_gain))


def _ssm_kernel(u_ref, wb_ref, wcr_ref, wci_ref, are_ref, aim_ref, d_ref, y_ref,
                ut, sre, sim, *, batch):
    nslab, nj = u_ref.shape[0], u_ref.shape[1]
    steps = nslab * TIME_TILE

    @pl.when(pl.program_id(0) == 0)
    def _():
        sre[...] = jnp.zeros_like(sre)
        sim[...] = jnp.zeros_like(sim)

    for c in range(nslab):
        for j in range(nj):
            src = u_ref.at[c, j]
            for t in range(TIME_TILE):
                ut[pl.ds((c * TIME_TILE + t) * batch, batch), j * LANES:(j + 1) * LANES] = (
                    src[pl.ds(t, batch, stride=TIME_TILE), :])

    sqrt_half = math.sqrt(0.5)

    def chunk(j):
        sl = slice(j * STATE_CHUNK, (j + 1) * STATE_CHUNK)
        lanes = slice(j * LANES, (j + 1) * LANES)
        uj = ut[:, lanes]
        bu = jnp.dot(uj.astype(BF16), wb_ref[j], preferred_element_type=F32)
        yield
        a_re = jnp.broadcast_to(are_ref[:, sl], (batch, STATE_CHUNK))
        a_im = jnp.broadcast_to(aim_ref[:, sl], (batch, STATE_CHUNK))
        x_re, x_im = sre[:, sl], sim[:, sl]
        hist_re, hist_im = [], []
        for t in range(steps):
            b_re = bu[t * batch:(t + 1) * batch, :STATE_CHUNK]
            b_im = bu[t * batch:(t + 1) * batch, STATE_CHUNK:]
            x_re, x_im = a_re * x_re - a_im * x_im + b_re, a_re * x_im + a_im * x_re + b_im
            hist_re.append(x_re.astype(BF16))
            hist_im.append(x_im.astype(BF16))
        sre[:, sl] = x_re
        sim[:, sl] = x_im
        yield
        y = (jnp.dot(jnp.concatenate(hist_re, axis=0), wcr_ref[j], preferred_element_type=F32)
             - jnp.dot(jnp.concatenate(hist_im, axis=0), wci_ref[j], preferred_element_type=F32))
        yield
        y = y + d_ref[:, lanes] * uj
        yg = 0.5 * y * (1.0 + lax.erf(y * sqrt_half))
        for c in range(nslab):
            dst = y_ref.at[c, j]
            for t in range(TIME_TILE):
                r0 = (c * TIME_TILE + t) * batch
                dst[pl.ds(t, batch, stride=TIME_TILE), :] = yg[r0:r0 + batch, :]

    _interleave([chunk(j) for j in range(nj)])


def _ssm(u_tiles, batch, w_b, w_cre, w_cim, a_re, a_im, d_skip):
    nslab_total, nj, _, _ = u_tiles.shape
    nslab = SSM_STEPS // TIME_TILE
    rows = SSM_STEPS * batch
    nstate = a_re.shape[1]
    tile_spec = pl.BlockSpec((nslab, nj, batch * TIME_TILE, LANES), lambda t: (t, 0, 0, 0))
    full = lambda a: pl.BlockSpec(a.shape, lambda t: (0,) * a.ndim)
    d2 = d_skip
    return pl.pallas_call(
        functools.partial(_ssm_kernel, batch=batch),
        grid=(nslab_total // nslab,),
        in_specs=[tile_spec, full(w_b), full(w_cre), full(w_cim), full(a_re), full(a_im), full(d2)],
        out_specs=tile_spec,
        out_shape=jax.ShapeDtypeStruct(u_tiles.shape, F32),
        scratch_shapes=[
            pltpu.VMEM((rows, nj * LANES), F32),
            pltpu.VMEM((batch, nstate), F32), pltpu.VMEM((batch, nstate), F32),
        ],
        compiler_params=pltpu.CompilerParams(
            dimension_semantics=("arbitrary",), vmem_limit_bytes=VMEM_LIMIT),
        name="ssm",
    )(u_tiles, w_b, w_cre, w_cim, a_re, a_im, d2)


def _out_kernel(attn_ref, y_ref, gs_ref, x_ref, p_ref, wglu32_ref, bglu_ref, wout32_ref, gple_ref,
                wg32_ref, wp32_ref, o_ref, wglu_ref, wout_ref, wg_ref, wp_ref):
    _cast_once([(wglu32_ref, wglu_ref), (wout32_ref, wout_ref), (wg32_ref, wg_ref), (wp32_ref, wp_ref)])
    nslab, nj = y_ref.shape[0], y_ref.shape[1]
    rows = OUT_CHUNK_ROWS
    aw = attn_ref.shape[-1]

    def chunk(c):
        r = pl.ds(c * rows, rows)
        slabs = slice(c * rows // TIME_TILE, (c + 1) * rows // TIME_TILE)
        yg = jnp.concatenate([y_ref[slabs, j].reshape(rows, LANES) for j in range(nj)], axis=1)
        glu = jnp.dot(yg.astype(BF16), wglu_ref[...], preferred_element_type=F32) + bglu_ref[...]
        pp = jnp.dot(p_ref[0, r, :].astype(BF16), wp_ref[...], preferred_element_type=F32)
        yield
        gs = gs_ref[0, r, :]
        ssm = yg * _sigmoid(glu) * (gs * _sigmoid(gs))
        h = (x_ref[0, r, :]
             + jnp.dot(attn_ref[0, r, :], wout_ref[:aw, :], preferred_element_type=F32)
             + jnp.dot(ssm.astype(BF16), wout_ref[aw:, :], preferred_element_type=F32))
        yield
        ms = jnp.mean(h * h, axis=-1, keepdims=True)
        hn = (h * lax.rsqrt(ms + EPS) * gple_ref[...]).astype(BF16)
        gate = _sigmoid(jnp.dot(hn, wg_ref[...], preferred_element_type=F32))
        yield
        o_ref[0, r, :] = h + gate * pp

    _interleave([chunk(c) for c in range(nslab * TIME_TILE // rows)])


def _out_stage(attn, y_tiles, gate_s, x, p, w_glu, b_glu, w_out, ple_gain, w_gate, w_proj):
    b, s, d = x.shape
    rows = OUT_ROW_TILE
    nj = y_tiles.shape[1]
    nat = lambda a: pl.BlockSpec((1, rows, a.shape[-1]), lambda i, t: (i, t, 0))
    full = lambda a: pl.BlockSpec(a.shape, lambda i, t: (0,) * a.ndim, pipeline_mode=pl.Buffered(1))
    y_spec = pl.BlockSpec((rows // TIME_TILE, nj, TIME_TILE, LANES), lambda i, t: (t, 0, i, 0))
    b2, g2 = b_glu, ple_gain
    return pl.pallas_call(
        _out_kernel,
        grid=(b, s // rows),
        in_specs=[nat(attn), y_spec, nat(gate_s), nat(x), nat(p), full(w_glu), full(b2), full(w_out),
                  full(g2), full(w_gate), full(w_proj)],
        out_specs=nat(x),
        out_shape=jax.ShapeDtypeStruct(x.shape, x.dtype),
        scratch_shapes=[pltpu.VMEM(w.shape, BF16) for w in (w_glu, w_out, w_gate, w_proj)],
        compiler_params=pltpu.CompilerParams(
            dimension_semantics=("arbitrary", "arbitrary"), vmem_limit_bytes=VMEM_LIMIT),
        name="outstage",
    )(attn, y_tiles, gate_s, x, p, w_glu, b2, w_out, g2, w_gate, w_proj)


def kernel(x, p, mix_norm, w_in, q_norm, k_norm, lambda_re, lambda_im, log_dt, b_re, b_im, c_re, c_im,
           d_skip, w_glu, b_glu, w_out, ple_norm, w_ple_gate, w_ple_proj):
    batch = x.shape[0]
    h = x
    for i in range(w_in.shape[0]):
        row = slice(i, i + 1)
        q, k, v, gate_a, u_tiles, gate_s = _inproj(h, mix_norm[row], w_in[i])
        a_re, a_im, w_b, w_cre, w_cim = _ssm_params(
            lambda_re[i], lambda_im[i], log_dt[i], b_re[i], b_im[i], c_re[i], c_im[i])
        attn = _attention(q, k, v, gate_a, q_norm[i], k_norm[i])
        y_tiles = _ssm(u_tiles, batch, w_b, w_cre, w_cim, a_re, a_im, d_skip[row])
        h = _out_stage(attn, y_tiles, gate_s, h, p[i], w_glu[i], b_glu[row], w_out[i], ple_norm[row],
                       w_ple_gate[i], w_ple_proj[i])
    return h
```

```python
import functools
import math

import jax
import jax.numpy as jnp
from jax import lax
from jax.experimental import pallas as pl
from jax.experimental.pallas import tpu as pltpu

F32 = jnp.float32
BF16 = jnp.bfloat16

HEAD_DIM = 64
SSM_GROUP = 16
SSM_STATE = 64
DILATED_CONFIGS = ((128, 1), (512, 4), (2048, 16))
EPS = 1e-6

LANES = 128
QBLK = 128
NEG = -1e30
LOG2E = math.log2(math.e)
IN_ROW_TILE = 1024
OUT_ROW_TILE = 1024
OUT_CHUNK_ROWS = 256
SSM_STEPS = 128
SSM_SUB = 32
TIME_TILE = 8
GROUPS_PER_CHUNK = LANES // SSM_GROUP
STATE_CHUNK = GROUPS_PER_CHUNK * SSM_STATE
VMEM_LIMIT = 56 * 1024 * 1024


def _sigmoid(x):
    return jax.nn.sigmoid(x)


def _advance(gens):
    for g in list(gens):
        try:
            next(g)
        except StopIteration:
            gens.remove(g)


def _cast_once(pairs):
    first = functools.reduce(jnp.logical_and, [pl.program_id(a) == 0 for a in range(2)])

    @pl.when(first)
    def _():
        for src, dst in pairs:
            for c in range(0, src.shape[1], 4 * LANES):
                dst[:, c:c + 4 * LANES] = src[:, c:c + 4 * LANES].astype(dst.dtype)


def _interleave(tasks):
    pending, active = list(tasks), []
    while pending or active:
        if pending:
            active.append(pending.pop(0))
        _advance(active)


def _inproj_kernel(x_ref, g_ref, w32_ref, q_ref, k_ref, v_ref, ga_ref, u_ref, gs_ref, w_ref, *, width):
    _cast_once([(w32_ref, w_ref)])
    x = x_ref[0]
    ms = jnp.mean(x * x, axis=-1, keepdims=True)
    xn = (x * lax.rsqrt(ms + EPS) * g_ref[...]).astype(BF16)

    def proj(idx):
        return jnp.dot(xn, w_ref[:, idx * width:(idx + 1) * width], preferred_element_type=F32)

    q_ref[0] = proj(0)
    k_ref[0] = proj(1)
    v_ref[0] = proj(2)
    ga_ref[0] = proj(3)
    u = proj(4)
    for c in range(u.shape[0] // TIME_TILE):
        for j in range(width // LANES):
            u_ref[c, j] = u[c * TIME_TILE:(c + 1) * TIME_TILE, j * LANES:(j + 1) * LANES]
    gs_ref[0] = proj(5)


def _inproj(x, gain, w):
    b, s, d = x.shape
    width = w.shape[1] // 6
    nj = width // LANES
    rows = IN_ROW_TILE
    nat = jax.ShapeDtypeStruct((b, s, width), F32)
    nat_spec = pl.BlockSpec((1, rows, width), lambda i, t: (i, t, 0))
    u_shape = jax.ShapeDtypeStruct((s // TIME_TILE, nj, b * TIME_TILE, LANES), F32)
    u_spec = pl.BlockSpec((rows // TIME_TILE, nj, TIME_TILE, LANES), lambda i, t: (t, 0, i, 0))
    return pl.pallas_call(
        functools.partial(_inproj_kernel, width=width),
        grid=(b, s // rows),
        in_specs=[
            pl.BlockSpec((1, rows, d), lambda i, t: (i, t, 0)),
            pl.BlockSpec((1, d), lambda i, t: (0, 0)),
            pl.BlockSpec(w.shape, lambda i, t: (0, 0), pipeline_mode=pl.Buffered(1)),
        ],
        out_specs=[nat_spec, nat_spec, nat_spec, nat_spec, u_spec, nat_spec],
        out_shape=[nat, nat, nat, nat, u_shape, nat],
        scratch_shapes=[pltpu.VMEM(w.shape, BF16)],
        compiler_params=pltpu.CompilerParams(
            dimension_semantics=("arbitrary", "arbitrary"), vmem_limit_bytes=VMEM_LIMIT),
        name="inproj",
    )(x, gain, w)


def _zoh(lam_re, lam_im, dt):
    mag = jnp.exp(lam_re * dt)
    a_re = mag * jnp.cos(lam_im * dt)
    a_im = mag * jnp.sin(lam_im * dt)
    den = lam_re * lam_re + lam_im * lam_im
    num_re = a_re - 1.0
    coef_re = (num_re * lam_re + a_im * lam_im) / den
    coef_im = (a_im * lam_re - num_re * lam_im) / den
    return a_re, a_im, coef_re, coef_im


def _zoh_kernel(lre_ref, lim_ref, ldt_ref, bre_ref, bim_ref, are_ref, aim_ref, bbre_ref, bbim_ref):
    g, n = lre_ref.shape
    c = bre_ref.shape[0] // g
    dt = jnp.exp(ldt_ref[...])
    a_re, a_im, coef_re, coef_im = _zoh(lre_ref[...], lim_ref[...], dt)
    are_ref[...] = a_re
    aim_ref[...] = a_im
    rows = lambda a: jnp.broadcast_to(a[:, None, :], (g, c, n)).reshape(g * c, n)
    coef_re, coef_im = rows(coef_re), rows(coef_im)
    b_re = bre_ref[...]
    b_im = bim_ref[...]
    bbre_ref[...] = coef_re * b_re - coef_im * b_im
    bbim_ref[...] = coef_re * b_im + coef_im * b_re


def _ssm_params(lam_re, lam_im, log_dt, b_re, b_im, c_re, c_im):
    g, n = lam_re.shape
    c = b_re.shape[-1]
    rows_gc = lambda b: b.transpose(0, 2, 1).reshape(g * c, n)
    small = jax.ShapeDtypeStruct((g, n), F32)
    tall = jax.ShapeDtypeStruct((g * c, n), F32)
    a_re, a_im, bb_re, bb_im = pl.pallas_call(
        _zoh_kernel, out_shape=[small, small, tall, tall], name="zoh",
    )(lam_re, lam_im, log_dt.reshape(g, 1), rows_gc(b_re), rows_gc(b_im))

    nch = g // GROUPS_PER_CHUNK

    def blockdiag(m, rows_per_group, cols_per_group):
        t = jnp.tile(m.reshape(nch, GROUPS_PER_CHUNK * rows_per_group, cols_per_group), (1, 1, GROUPS_PER_CHUNK))
        rg = lax.broadcasted_iota(jnp.int32, t.shape, 1) // rows_per_group
        cg = lax.broadcasted_iota(jnp.int32, t.shape, 2) // cols_per_group
        return jnp.where(rg == cg, t, 0.0)

    w_b = jnp.concatenate([blockdiag(bb_re, c, n), blockdiag(bb_im, c, n)], axis=-1).astype(BF16)
    rows_gn = lambda cc: cc.transpose(0, 2, 1).reshape(g * n, c)
    return (a_re.reshape(1, g * n), a_im.reshape(1, g * n), w_b,
            blockdiag(rows_gn(c_re), n, c).astype(BF16), blockdiag(rows_gn(c_im), n, c).astype(BF16))


def _attn_kernel(q_ref, k_ref, v_ref, ga_ref, gq_ref, gk_ref, o_ref,
                 qn, kn, ops_a, ops_b, res, out32, bias_full, bias_first, bias_mod4):
    seq = qn.shape[0]
    nblk = seq // QBLK
    per_class = nblk // 4
    assert DILATED_CONFIGS == ((128, 1), (512, 4), (2048, 16)) and nblk == 16
    lane = lax.broadcasted_iota(jnp.int32, (1, LANES), 1)
    head0 = lane < HEAD_DIM
    rr = lax.broadcasted_iota(jnp.int32, (2 * LANES, 2 * LANES), 0) // HEAD_DIM
    cc = lax.broadcasted_iota(jnp.int32, (2 * LANES, 2 * LANES), 1) // HEAD_DIM
    head_mean = jnp.where(rr == cc, 1.0 / HEAD_DIM, 0.0).astype(BF16)
    gq = gq_ref[...] * (HEAD_DIM ** -0.5 * LOG2E)
    gk = gk_ref[...]
    one0 = jnp.where(head0, 1.0, 0.0).astype(BF16)
    one1 = jnp.where(head0, 0.0, 1.0).astype(BF16)

    def blk(i, n=QBLK):
        return pl.ds(i * QBLK, n)

    def write_operands(ops, b, qv, kv, vv):
        rows = blk(b)
        vb = vv.astype(BF16)
        kb = kv.astype(BF16)
        zero = jnp.zeros_like(vb)
        ops[0, rows, :] = qv.astype(BF16)
        ops[1, rows, :] = jnp.where(head0, vb, zero)
        ops[2, rows, :] = jnp.where(head0, zero, vb)
        ops[3, rows, :] = jnp.where(head0, kb, zero)
        ops[4, rows, :] = jnp.where(head0, zero, kb)

    def norm_block(b):
        rows = blk(b)
        q = q_ref[0, rows, :]
        k = k_ref[0, rows, :]
        ms = jnp.dot(jnp.concatenate([(q * q).astype(BF16), (k * k).astype(BF16)], axis=1), head_mean,
                     preferred_element_type=F32)
        qv = q * lax.rsqrt(ms[:, :LANES] + EPS) * gq
        kv = k * lax.rsqrt(ms[:, LANES:] + EPS) * gk
        qn[rows, :] = qv
        kn[rows, :] = kv
        write_operands(ops_a, b, qv, kv, v_ref[0, rows, :])

    def to_mod4(c):
        src = pl.ds((c // per_class) + (c % per_class) * (4 * QBLK), QBLK, stride=4)
        write_operands(ops_b, c, qn[src, :], kn[src, :], v_ref.at[0][src, :])

    @pl.when((pl.program_id(0) == 0) & (pl.program_id(1) == 0))
    def _():
        qi = lax.broadcasted_iota(jnp.int32, (QBLK, 2 * QBLK), 0)
        kj = lax.broadcasted_iota(jnp.int32, (QBLK, 2 * QBLK), 1)
        bias_full[...] = jnp.where((kj >= qi) & (kj <= qi + QBLK), 0.0, NEG)
        qi1 = lax.broadcasted_iota(jnp.int32, (QBLK, QBLK), 0)
        kj1 = lax.broadcasted_iota(jnp.int32, (QBLK, QBLK), 1)
        bias_first[...] = jnp.where(kj1 <= qi1, 0.0, NEG)
        same4 = ((qi1 - kj1) & 3) == 0
        far = jnp.where(same4, 0.0, NEG)
        prev = jnp.where(same4, jnp.where(kj1 >= qi1, 1.0, 0.0), jnp.where(kj1 >= qi1, 0.0, NEG))
        cur = jnp.where(kj1 <= qi1, jnp.where(same4, 1.0, 0.0), NEG)
        for t, tile in enumerate((far, far, prev, cur)):
            bias_mod4[:, t * QBLK:(t + 1) * QBLK] = tile

    def unit(ops, b, k0, nk, bias, out):
        krows = pl.ds(k0, nk)
        kcat = jnp.concatenate([ops[3, krows, :], ops[4, krows, :]], axis=0)
        s2 = lax.dot_general(ops[0, blk(b), :], kcat, (((1,), (1,)), ((), ())),
                             preferred_element_type=F32)
        yield
        ss, ms = [], []
        for h in range(2):
            s = s2[:, h * nk:(h + 1) * nk] + bias
            ms.append(jnp.max(s, axis=-1, keepdims=True))
            ss.append(s)
        yield
        es = [jnp.exp2(s - m).astype(BF16) for s, m in zip(ss, ms)]
        yield
        rhs = jnp.concatenate([
            jnp.concatenate([ops[1, krows, :], jnp.broadcast_to(one0, (nk, LANES))], axis=1),
            jnp.concatenate([ops[2, krows, :], jnp.broadcast_to(one1, (nk, LANES))], axis=1)], axis=0)
        o = jnp.dot(jnp.concatenate(es, axis=1), rhs, preferred_element_type=F32)
        yield
        out.extend([jnp.where(head0, ms[0], ms[1]), o[:, :LANES], o[:, LANES:]])

    def unit_natural(b):
        part = []
        if b == 0:
            yield from unit(ops_a, b, 0, QBLK, bias_first[...], part)
        else:
            yield from unit(ops_a, b, (b - 1) * QBLK, 2 * QBLK, bias_full[...], part)
        for k in range(3):
            res[k, blk(b), :] = part[k]
        yield
        to_mod4(b)

    def unit_mod4(b):
        r, i = b // per_class, b % per_class
        rows = pl.ds(r + i * (4 * QBLK), QBLK, stride=4)
        seg = r * per_class * QBLK
        near = min(i + 1, 2)
        part_near, part_far = [], []
        gens = [unit(ops_b, b, seg + (i + 1 - near) * QBLK, near * QBLK,
                     bias_mod4[:, (per_class - near) * QBLK:], part_near)]
        if i >= 2:
            gens.append(unit(ops_b, b, seg, (i - 1) * QBLK, bias_mod4[:, :(i - 1) * QBLK], part_far))
        while gens:
            _advance(gens)
            yield
        parts = [[res.at[k][rows, :] for k in range(3)], part_near] + ([part_far] if part_far else [])
        mx = functools.reduce(jnp.maximum, [p[0] for p in parts])
        ws = [jnp.exp2(p[0] - mx) for p in parts]
        num = functools.reduce(jnp.add, [w * p[1] for w, p in zip(ws, parts)])
        den = functools.reduce(jnp.add, [w * p[2] for w, p in zip(ws, parts)])
        g = ga_ref.at[0][rows, :]
        out32[rows, :] = num / den * (g * _sigmoid(g))

    def cast_out(i):
        o_ref[0, blk(i), :] = out32[blk(i), :].astype(o_ref.dtype)

    for b in range(nblk):
        norm_block(b)
    _interleave([unit_natural(b) for b in range(nblk)])
    _interleave([unit_mod4(b) for b in range(nblk)])
    for i in range(nblk):
        cast_out(i)


def _attention(q, k, v, gate, q_gain, k_gain):
    b, s, width = q.shape
    pairs = width // LANES
    spec = pl.BlockSpec((1, s, LANES), lambda i, h: (i, 0, h))
    gspec = pl.BlockSpec((1, LANES), lambda i, h: (0, 0))
    tile2 = lambda g: jnp.concatenate([g, g]).reshape(1, LANES)
    return pl.pallas_call(
        _attn_kernel,
        grid=(b, pairs),
        in_specs=[spec, spec, spec, spec, gspec, gspec],
        out_specs=spec,
        out_shape=jax.ShapeDtypeStruct((b, s, width), BF16),
        scratch_shapes=[
            pltpu.VMEM((s, LANES), F32), pltpu.VMEM((s, LANES), F32),
            pltpu.VMEM((5, s, LANES), BF16),
            pltpu.VMEM((5, s, LANES), BF16),
            pltpu.VMEM((3, s, LANES), F32),
            pltpu.VMEM((s, LANES), F32),
            pltpu.VMEM((QBLK, 2 * QBLK), F32), pltpu.VMEM((QBLK, QBLK), F32), pltpu.VMEM((QBLK, 4 * QBLK), F32),
        ],
        compiler_params=pltpu.CompilerParams(
            dimension_semantics=("arbitrary", "arbitrary"), vmem_limit_bytes=VMEM_LIMIT),
        name="attn",
    )(q, k, v, gate, tile2(q_gain), tile2(k_gain))


def _ssm_kernel(u_ref, wb_ref, wcr_ref, wci_ref, are_ref, aim_ref, d_ref, y_ref,
                ut, sre, sim, *, batch):
    nslab, nj = u_ref.shape[0], u_ref.shape[1]
    steps = SSM_SUB
    nsub = nslab * TIME_TILE // steps

    @pl.when(pl.program_id(0) == 0)
    def _():
        sre[...] = jnp.zeros_like(sre)
        sim[...] = jnp.zeros_like(sim)

    for c in range(nslab):
        for j in range(nj):
            src = u_ref.at[c, j]
            for t in range(TIME_TILE):
                ut[pl.ds((c * TIME_TILE + t) * batch, batch), j * LANES:(j + 1) * LANES] = (
                    src[pl.ds(t, batch, stride=TIME_TILE), :])

    sqrt_half = math.sqrt(0.5)

    def chunk(j, sub):
        r0s = sub * steps * batch
        sl = slice(j * STATE_CHUNK, (j + 1) * STATE_CHUNK)
        lanes = slice(j * LANES, (j + 1) * LANES)
        uj = ut[r0s:r0s + steps * batch, lanes]
        bu = jnp.dot(uj.astype(BF16), wb_ref[j], preferred_element_type=F32)
        yield
        a_re = jnp.broadcast_to(are_ref[:, sl], (batch, STATE_CHUNK))
        a_im = jnp.broadcast_to(aim_ref[:, sl], (batch, STATE_CHUNK))
        x_re, x_im = sre[:, sl], sim[:, sl]
        hist_re, hist_im = [], []
        for t in range(steps):
            b_re = bu[t * batch:(t + 1) * batch, :STATE_CHUNK]
            b_im = bu[t * batch:(t + 1) * batch, STATE_CHUNK:]
            x_re, x_im = a_re * x_re - a_im * x_im + b_re, a_re * x_im + a_im * x_re + b_im
            hist_re.append(x_re.astype(BF16))
            hist_im.append(x_im.astype(BF16))
        sre[:, sl] = x_re
        sim[:, sl] = x_im
        yield
        y = (jnp.dot(jnp.concatenate(hist_re, axis=0), wcr_ref[j], preferred_element_type=F32)
             - jnp.dot(jnp.concatenate(hist_im, axis=0), wci_ref[j], preferred_element_type=F32))
        yield
        y = y + d_ref[:, lanes] * uj
        yg = 0.5 * y * (1.0 + lax.erf(y * sqrt_half))
        for c in range(steps // TIME_TILE):
            dst = y_ref.at[sub * (steps // TIME_TILE) + c, j]
            for t in range(TIME_TILE):
                r0 = (c * TIME_TILE + t) * batch
                dst[pl.ds(t, batch, stride=TIME_TILE), :] = yg[r0:r0 + batch, :]

    _interleave([chunk(j, sub) for sub in range(nsub) for j in range(nj)])


def _ssm(u_tiles, batch, w_b, w_cre, w_cim, a_re, a_im, d_skip):
    nslab_total, nj, _, _ = u_tiles.shape
    nslab = SSM_STEPS // TIME_TILE
    rows = SSM_STEPS * batch
    nstate = a_re.shape[1]
    tile_spec = pl.BlockSpec((nslab, nj, batch * TIME_TILE, LANES), lambda t: (t, 0, 0, 0))
    full = lambda a: pl.BlockSpec(a.shape, lambda t: (0,) * a.ndim)
    d2 = d_skip
    return pl.pallas_call(
        functools.partial(_ssm_kernel, batch=batch),
        grid=(nslab_total // nslab,),
        in_specs=[tile_spec, full(w_b), full(w_cre), full(w_cim), full(a_re), full(a_im), full(d2)],
        out_specs=tile_spec,
        out_shape=jax.ShapeDtypeStruct(u_tiles.shape, F32),
        scratch_shapes=[
            pltpu.VMEM((rows, nj * LANES), F32),
            pltpu.VMEM((batch, nstate), F32), pltpu.VMEM((batch, nstate), F32),
        ],
        compiler_params=pltpu.CompilerParams(
            dimension_semantics=("arbitrary",), vmem_limit_bytes=VMEM_LIMIT),
        name="ssm",
    )(u_tiles, w_b, w_cre, w_cim, a_re, a_im, d2)


def _out_kernel(attn_ref, y_ref, gs_ref, x_ref, p_ref, wglu32_ref, bglu_ref, wout32_ref, gple_ref,
                wg32_ref, wp32_ref, o_ref, wglu_ref, wout_ref, wg_ref, wp_ref):
    _cast_once([(wglu32_ref, wglu_ref), (wout32_ref, wout_ref), (wg32_ref, wg_ref), (wp32_ref, wp_ref)])
    nslab, nj = y_ref.shape[0], y_ref.shape[1]
    rows = OUT_CHUNK_ROWS
    aw = attn_ref.shape[-1]

    def chunk(c):
        r = pl.ds(c * rows, rows)
        slabs = slice(c * rows // TIME_TILE, (c + 1) * rows // TIME_TILE)
        yg = jnp.concatenate([y_ref[slabs, j].reshape(rows, LANES) for j in range(nj)], axis=1)
        glu = jnp.dot(yg.astype(BF16), wglu_ref[...], preferred_element_type=F32) + bglu_ref[...]
        pp = jnp.dot(p_ref[0, r, :].astype(BF16), wp_ref[...], preferred_element_type=F32)
        yield
        gs = gs_ref[0, r, :]
        ssm = yg * _sigmoid(glu) * (gs * _sigmoid(gs))
        h = (x_ref[0, r, :]
             + jnp.dot(attn_ref[0, r, :], wout_ref[:aw, :], preferred_element_type=F32)
             + jnp.dot(ssm.astype(BF16), wout_ref[aw:, :], preferred_element_type=F32))
        yield
        ms = jnp.mean(h * h, axis=-1, keepdims=True)
        hn = (h * lax.rsqrt(ms + EPS) * gple_ref[...]).astype(BF16)
        gate = _sigmoid(jnp.dot(hn, wg_ref[...], preferred_element_type=F32))
        yield
        o_ref[0, r, :] = h + gate * pp

    _interleave([chunk(c) for c in range(nslab * TIME_TILE // rows)])


def _out_stage(attn, y_tiles, gate_s, x, p, w_glu, b_glu, w_out, ple_gain, w_gate, w_proj):
    b, s, d = x.shape
    rows = OUT_ROW_TILE
    nj = y_tiles.shape[1]
    nat = lambda a: pl.BlockSpec((1, rows, a.shape[-1]), lambda i, t: (i, t, 0))
    full = lambda a: pl.BlockSpec(a.shape, lambda i, t: (0,) * a.ndim, pipeline_mode=pl.Buffered(1))
    y_spec = pl.BlockSpec((rows // TIME_TILE, nj, TIME_TILE, LANES), lambda i, t: (t, 0, i, 0))
    b2, g2 = b_glu, ple_gain
    return pl.pallas_call(
        _out_kernel,
        grid=(b, s // rows),
        in_specs=[nat(attn), y_spec, nat(gate_s), nat(x), nat(p), full(w_glu), full(b2), full(w_out),
                  full(g2), full(w_gate), full(w_proj)],
        out_specs=nat(x),
        out_shape=jax.ShapeDtypeStruct(x.shape, x.dtype),
        scratch_shapes=[pltpu.VMEM(w.shape, BF16) for w in (w_glu, w_out, w_gate, w_proj)],
        compiler_params=pltpu.CompilerParams(
            dimension_semantics=("arbitrary", "arbitrary"), vmem_limit_bytes=VMEM_LIMIT),
        name="outstage",
    )(attn, y_tiles, gate_s, x, p, w_glu, b2, w_out, g2, w_gate, w_proj)


def kernel(x, p, mix_norm, w_in, q_norm, k_norm, lambda_re, lambda_im, log_dt, b_re, b_im, c_re, c_im,
           d_skip, w_glu, b_glu, w_out, ple_norm, w_ple_gate, w_ple_proj):
    batch = x.shape[0]
    h = x
    for i in range(w_in.shape[0]):
        row = slice(i, i + 1)
        q, k, v, gate_a, u_tiles, gate_s = _inproj(h, mix_norm[row], w_in[i])
        a_re, a_im, w_b, w_cre, w_cim = _ssm_params(
            lambda_re[i], lambda_im[i], log_dt[i], b_re[i], b_im[i], c_re[i], c_im[i])
        attn = _attention(q, k, v, gate_a, q_norm[i], k_norm[i])
        y_tiles = _ssm(u_tiles, batch, w_b, w_cre, w_cim, a_re, a_im, d_skip[row])
        h = _out_stage(attn, y_tiles, gate_s, h, p[i], w_glu[i], b_glu[row], w_out[i], ple_norm[row],
                       w_ple_gate[i], w_ple_proj[i])
    return h
```

```python
import functools
import math

import jax
import jax.numpy as jnp
from jax import lax
from jax.experimental import pallas as pl
from jax.experimental.pallas import tpu as pltpu

F32 = jnp.float32
BF16 = jnp.bfloat16

HEAD_DIM = 64
SSM_GROUP = 16
SSM_STATE = 64
DILATED_CONFIGS = ((128, 1), (512, 4), (2048, 16))
EPS = 1e-6

LANES = 128
QBLK = 128
NEG = -1e30
LOG2E = math.log2(math.e)
IN_ROW_TILE = 1024
OUT_ROW_TILE = 1024
OUT_CHUNK_ROWS = 256
SSM_STEPS = 128
SSM_SUB = 32
TIME_TILE = 8
GROUPS_PER_CHUNK = LANES // SSM_GROUP
STATE_CHUNK = GROUPS_PER_CHUNK * SSM_STATE
VMEM_LIMIT = 56 * 1024 * 1024


def _sigmoid(x):
    return jax.nn.sigmoid(x)


def _advance(gens):
    for g in list(gens):
        try:
            next(g)
        except StopIteration:
            gens.remove(g)


def _cast_once(pairs):
    first = functools.reduce(jnp.logical_and, [pl.program_id(a) == 0 for a in range(2)])

    @pl.when(first)
    def _():
        for src, dst in pairs:
            for c in range(0, src.shape[1], 4 * LANES):
                dst[:, c:c + 4 * LANES] = src[:, c:c + 4 * LANES].astype(dst.dtype)


def _interleave(tasks):
    pending, active = list(tasks), []
    while pending or active:
        if pending:
            active.append(pending.pop(0))
        _advance(active)


def _inproj_kernel(x_ref, g_ref, w32_ref, q_ref, k_ref, v_ref, ga_ref, u_ref, gs_ref, w_ref, *, width):
    _cast_once([(w32_ref, w_ref)])
    x = x_ref[0]
    ms = jnp.mean(x * x, axis=-1, keepdims=True)
    xn = (x * lax.rsqrt(ms + EPS) * g_ref[...]).astype(BF16)

    def proj(idx):
        return jnp.dot(xn, w_ref[:, idx * width:(idx + 1) * width], preferred_element_type=F32)

    q_ref[0] = proj(0)
    k_ref[0] = proj(1)
    v_ref[0] = proj(2)
    ga_ref[0] = proj(3)
    u = proj(4)
    for c in range(u.shape[0] // TIME_TILE):
        for j in range(width // LANES):
            u_ref[c, j] = u[c * TIME_TILE:(c + 1) * TIME_TILE, j * LANES:(j + 1) * LANES]
    gs_ref[0] = proj(5)


def _inproj(x, gain, w):
    b, s, d = x.shape
    width = w.shape[1] // 6
    nj = width // LANES
    rows = IN_ROW_TILE
    nat = jax.ShapeDtypeStruct((b, s, width), F32)
    nat_spec = pl.BlockSpec((1, rows, width), lambda i, t: (i, t, 0))
    u_shape = jax.ShapeDtypeStruct((s // TIME_TILE, nj, b * TIME_TILE, LANES), F32)
    u_spec = pl.BlockSpec((rows // TIME_TILE, nj, TIME_TILE, LANES), lambda i, t: (t, 0, i, 0))
    return pl.pallas_call(
        functools.partial(_inproj_kernel, width=width),
        grid=(b, s // rows),
        in_specs=[
            pl.BlockSpec((1, rows, d), lambda i, t: (i, t, 0)),
            pl.BlockSpec((1, d), lambda i, t: (0, 0)),
            pl.BlockSpec(w.shape, lambda i, t: (0, 0), pipeline_mode=pl.Buffered(1)),
        ],
        out_specs=[nat_spec, nat_spec, nat_spec, nat_spec, u_spec, nat_spec],
        out_shape=[nat, nat, nat, nat, u_shape, nat],
        scratch_shapes=[pltpu.VMEM(w.shape, BF16)],
        compiler_params=pltpu.CompilerParams(
            dimension_semantics=("arbitrary", "arbitrary"), vmem_limit_bytes=VMEM_LIMIT),
        name="inproj",
    )(x, gain, w)


def _zoh(lam_re, lam_im, dt):
    mag = jnp.exp(lam_re * dt)
    a_re = mag * jnp.cos(lam_im * dt)
    a_im = mag * jnp.sin(lam_im * dt)
    den = lam_re * lam_re + lam_im * lam_im
    num_re = a_re - 1.0
    coef_re = (num_re * lam_re + a_im * lam_im) / den
    coef_im = (a_im * lam_re - num_re * lam_im) / den
    return a_re, a_im, coef_re, coef_im


def _zoh_kernel(lre_ref, lim_ref, ldt_ref, bre_ref, bim_ref, are_ref, aim_ref, bbre_ref, bbim_ref):
    g, n = lre_ref.shape
    c = bre_ref.shape[0] // g
    dt = jnp.exp(ldt_ref[...])
    a_re, a_im, coef_re, coef_im = _zoh(lre_ref[...], lim_ref[...], dt)
    are_ref[...] = a_re
    aim_ref[...] = a_im
    rows = lambda a: jnp.broadcast_to(a[:, None, :], (g, c, n)).reshape(g * c, n)
    coef_re, coef_im = rows(coef_re), rows(coef_im)
    b_re = bre_ref[...]
    b_im = bim_ref[...]
    bbre_ref[...] = coef_re * b_re - coef_im * b_im
    bbim_ref[...] = coef_re * b_im + coef_im * b_re


def _ssm_params(lam_re, lam_im, log_dt, b_re, b_im, c_re, c_im):
    g, n = lam_re.shape
    c = b_re.shape[-1]
    rows_gc = lambda b: b.transpose(0, 2, 1).reshape(g * c, n)
    small = jax.ShapeDtypeStruct((g, n), F32)
    tall = jax.ShapeDtypeStruct((g * c, n), F32)
    a_re, a_im, bb_re, bb_im = pl.pallas_call(
        _zoh_kernel, out_shape=[small, small, tall, tall], name="zoh",
    )(lam_re, lam_im, log_dt.reshape(g, 1), rows_gc(b_re), rows_gc(b_im))

    nch = g // GROUPS_PER_CHUNK

    def blockdiag(m, rows_per_group, cols_per_group):
        t = jnp.tile(m.reshape(nch, GROUPS_PER_CHUNK * rows_per_group, cols_per_group), (1, 1, GROUPS_PER_CHUNK))
        rg = lax.broadcasted_iota(jnp.int32, t.shape, 1) // rows_per_group
        cg = lax.broadcasted_iota(jnp.int32, t.shape, 2) // cols_per_group
        return jnp.where(rg == cg, t, 0.0)

    w_b = jnp.concatenate([blockdiag(bb_re, c, n), blockdiag(bb_im, c, n)], axis=-1).astype(BF16)
    rows_gn = lambda cc: cc.transpose(0, 2, 1).reshape(g * n, c)
    return (a_re.reshape(1, g * n), a_im.reshape(1, g * n), w_b,
            blockdiag(rows_gn(c_re), n, c).astype(BF16), blockdiag(rows_gn(c_im), n, c).astype(BF16))


def _attn_kernel(q_ref, k_ref, v_ref, ga_ref, gq_ref, gk_ref, o_ref,
                 qn, kn, ops_a, ops_b, res, out32, bias_full, bias_first, bias_mod4):
    seq = qn.shape[0]
    nblk = seq // QBLK
    per_class = nblk // 4
    assert DILATED_CONFIGS == ((128, 1), (512, 4), (2048, 16)) and nblk == 16
    lane = lax.broadcasted_iota(jnp.int32, (1, LANES), 1)
    head0 = lane < HEAD_DIM
    rr = lax.broadcasted_iota(jnp.int32, (2 * LANES, 2 * LANES), 0) // HEAD_DIM
    cc = lax.broadcasted_iota(jnp.int32, (2 * LANES, 2 * LANES), 1) // HEAD_DIM
    head_mean = jnp.where(rr == cc, 1.0 / HEAD_DIM, 0.0).astype(BF16)
    gq = gq_ref[...] * (HEAD_DIM ** -0.5 * LOG2E)
    gk = gk_ref[...]
    one0 = jnp.where(head0, 1.0, 0.0).astype(BF16)
    one1 = jnp.where(head0, 0.0, 1.0).astype(BF16)

    def blk(i, n=QBLK):
        return pl.ds(i * QBLK, n)

    def write_operands(ops, b, qv, kv, vv):
        rows = blk(b)
        vb = vv.astype(BF16)
        kb = kv.astype(BF16)
        zero = jnp.zeros_like(vb)
        ops[0, rows, :] = qv.astype(BF16)
        ops[1, rows, :] = jnp.where(head0, vb, zero)
        ops[2, rows, :] = jnp.where(head0, zero, vb)
        ops[3, rows, :] = jnp.where(head0, kb, zero)
        ops[4, rows, :] = jnp.where(head0, zero, kb)

    def norm_block(b):
        rows = blk(b)
        q = q_ref[0, rows, :]
        k = k_ref[0, rows, :]
        ms = jnp.dot(jnp.concatenate([(q * q).astype(BF16), (k * k).astype(BF16)], axis=1), head_mean,
                     preferred_element_type=F32)
        qv = q * lax.rsqrt(ms[:, :LANES] + EPS) * gq
        kv = k * lax.rsqrt(ms[:, LANES:] + EPS) * gk
        qn[rows, :] = qv
        kn[rows, :] = kv
        write_operands(ops_a, b, qv, kv, v_ref[0, rows, :])

    def to_mod4(c):
        src = pl.ds((c // per_class) + (c % per_class) * (4 * QBLK), QBLK, stride=4)
        write_operands(ops_b, c, qn[src, :], kn[src, :], v_ref.at[0][src, :])

    @pl.when((pl.program_id(0) == 0) & (pl.program_id(1) == 0))
    def _():
        qi = lax.broadcasted_iota(jnp.int32, (QBLK, 2 * QBLK), 0)
        kj = lax.broadcasted_iota(jnp.int32, (QBLK, 2 * QBLK), 1)
        bias_full[...] = jnp.where((kj >= qi) & (kj <= qi + QBLK), 0.0, NEG)
        qi1 = lax.broadcasted_iota(jnp.int32, (QBLK, QBLK), 0)
        kj1 = lax.broadcasted_iota(jnp.int32, (QBLK, QBLK), 1)
        bias_first[...] = jnp.where(kj1 <= qi1, 0.0, NEG)
        same4 = ((qi1 - kj1) & 3) == 0
        far = jnp.where(same4, 0.0, NEG)
        prev = jnp.where(same4, jnp.where(kj1 >= qi1, 1.0, 0.0), jnp.where(kj1 >= qi1, 0.0, NEG))
        cur = jnp.where(kj1 <= qi1, jnp.where(same4, 1.0, 0.0), NEG)
        for t, tile in enumerate((far, far, prev, cur)):
            bias_mod4[:, t * QBLK:(t + 1) * QBLK] = tile

    def unit(ops, b, k0, nk, bias, out):
        krows = pl.ds(k0, nk)
        kcat = jnp.concatenate([ops[3, krows, :], ops[4, krows, :]], axis=0)
        s2 = lax.dot_general(ops[0, blk(b), :], kcat, (((1,), (1,)), ((), ())),
                             preferred_element_type=F32)
        yield
        ss, ms = [], []
        for h in range(2):
            s = s2[:, h * nk:(h + 1) * nk] + bias
            ms.append(jnp.max(s, axis=-1, keepdims=True))
            ss.append(s)
        yield
        es = [jnp.exp2(s - m).astype(BF16) for s, m in zip(ss, ms)]
        yield
        rhs = jnp.concatenate([
            jnp.concatenate([ops[1, krows, :], jnp.broadcast_to(one0, (nk, LANES))], axis=1),
            jnp.concatenate([ops[2, krows, :], jnp.broadcast_to(one1, (nk, LANES))], axis=1)], axis=0)
        o = jnp.dot(jnp.concatenate(es, axis=1), rhs, preferred_element_type=F32)
        yield
        out.extend([jnp.where(head0, ms[0], ms[1]), o[:, :LANES], o[:, LANES:]])

    def unit_natural(b):
        part = []
        if b == 0:
            yield from unit(ops_a, b, 0, QBLK, bias_first[...], part)
        else:
            yield from unit(ops_a, b, (b - 1) * QBLK, 2 * QBLK, bias_full[...], part)
        for k in range(3):
            res[k, blk(b), :] = part[k]
        yield
        to_mod4(b)

    def unit_mod4(b):
        r, i = b // per_class, b % per_class
        rows = pl.ds(r + i * (4 * QBLK), QBLK, stride=4)
        seg = r * per_class * QBLK
        near = min(i + 1, 2)
        part_near, part_far = [], []
        gens = [unit(ops_b, b, seg + (i + 1 - near) * QBLK, near * QBLK,
                     bias_mod4[:, (per_class - near) * QBLK:], part_near)]
        if i >= 2:
            gens.append(unit(ops_b, b, seg, (i - 1) * QBLK, bias_mod4[:, :(i - 1) * QBLK], part_far))
        while gens:
            _advance(gens)
            yield
        parts = [[res.at[k][rows, :] for k in range(3)], part_near] + ([part_far] if part_far else [])
        mx = functools.reduce(jnp.maximum, [p[0] for p in parts])
        ws = [jnp.exp2(p[0] - mx) for p in parts]
        num = functools.reduce(jnp.add, [w * p[1] for w, p in zip(ws, parts)])
        den = functools.reduce(jnp.add, [w * p[2] for w, p in zip(ws, parts)])
        g = ga_ref.at[0][rows, :]
        out32[rows, :] = num / den * (g * _sigmoid(g))

    def cast_out(i):
        o_ref[0, blk(i), :] = out32[blk(i), :].astype(o_ref.dtype)

    for b in range(nblk):
        norm_block(b)
    _interleave([unit_natural(b) for b in range(nblk)])
    _interleave([unit_mod4(b) for b in range(nblk)])
    for i in range(nblk):
        cast_out(i)


def _attention(q, k, v, gate, q_gain, k_gain):
    b, s, width = q.shape
    pairs = width // LANES
    spec = pl.BlockSpec((1, s, LANES), lambda i, h: (i, 0, h))
    gspec = pl.BlockSpec((1, LANES), lambda i, h: (0, 0))
    tile2 = lambda g: jnp.concatenate([g, g]).reshape(1, LANES)
    return pl.pallas_call(
        _attn_kernel,
        grid=(b, pairs),
        in_specs=[spec, spec, spec, spec, gspec, gspec],
        out_specs=spec,
        out_shape=jax.ShapeDtypeStruct((b, s, width), BF16),
        scratch_shapes=[
            pltpu.VMEM((s, LANES), F32), pltpu.VMEM((s, LANES), F32),
            pltpu.VMEM((5, s, LANES), BF16),
            pltpu.VMEM((5, s, LANES), BF16),
            pltpu.VMEM((3, s, LANES), F32),
            pltpu.VMEM((s, LANES), F32),
            pltpu.VMEM((QBLK, 2 * QBLK), F32), pltpu.VMEM((QBLK, QBLK), F32), pltpu.VMEM((QBLK, 4 * QBLK), F32),
        ],
        compiler_params=pltpu.CompilerParams(
            dimension_semantics=("arbitrary", "arbitrary"), vmem_limit_bytes=VMEM_LIMIT),
        name="attn",
    )(q, k, v, gate, tile2(q_gain), tile2(k_gain))


def _ssm_kernel(u_ref, wb_ref, wcr_ref, wci_ref, are_ref, aim_ref, d_ref, y_ref,
                ut, sre, sim, *, batch):
    nslab, nj = u_ref.shape[0], u_ref.shape[1]
    steps = SSM_SUB
    nsub = nslab * TIME_TILE // steps

    @pl.when(pl.program_id(0) == 0)
    def _():
        sre[...] = jnp.zeros_like(sre)
        sim[...] = jnp.zeros_like(sim)

    for c in range(nslab):
        for j in range(nj):
            src = u_ref.at[c, j]
            for t in range(TIME_TILE):
                ut[pl.ds((c * TIME_TILE + t) * batch, batch), j * LANES:(j + 1) * LANES] = (
                    src[pl.ds(t, batch, stride=TIME_TILE), :])

    sqrt_half = math.sqrt(0.5)

    def chunk(j, sub):
        r0s = sub * steps * batch
        sl = slice(j * STATE_CHUNK, (j + 1) * STATE_CHUNK)
        lanes = slice(j * LANES, (j + 1) * LANES)
        uj = ut[r0s:r0s + steps * batch, lanes]
        bu = jnp.dot(uj.astype(BF16), wb_ref[j], preferred_element_type=F32)
        yield
        a_re = jnp.broadcast_to(are_ref[:, sl], (batch, STATE_CHUNK))
        a_im = jnp.broadcast_to(aim_ref[:, sl], (batch, STATE_CHUNK))
        x_re, x_im = sre[:, sl], sim[:, sl]
        hist_re, hist_im = [], []
        for t in range(steps):
            b_re = bu[t * batch:(t + 1) * batch, :STATE_CHUNK]
            b_im = bu[t * batch:(t + 1) * batch, STATE_CHUNK:]
            x_re, x_im = a_re * x_re - a_im * x_im + b_re, a_re * x_im + a_im * x_re + b_im
            hist_re.append(x_re.astype(BF16))
            hist_im.append(x_im.astype(BF16))
        sre[:, sl] = x_re
        sim[:, sl] = x_im
        yield
        y = (jnp.dot(jnp.concatenate(hist_re, axis=0), wcr_ref[j], preferred_element_type=F32)
             - jnp.dot(jnp.concatenate(hist_im, axis=0), wci_ref[j], preferred_element_type=F32))
        yield
        y = y + d_ref[:, lanes] * uj
        yg = 0.5 * y * (1.0 + lax.erf(y * sqrt_half))
        for c in range(steps // TIME_TILE):
            dst = y_ref.at[sub * (steps // TIME_TILE) + c, j]
            for t in range(TIME_TILE):
                r0 = (c * TIME_TILE + t) * batch
                dst[pl.ds(t, batch, stride=TIME_TILE), :] = yg[r0:r0 + batch, :]

    _interleave([chunk(j, sub) for sub in range(nsub) for j in range(nj)])


def _ssm(u_tiles, batch, w_b, w_cre, w_cim, a_re, a_im, d_skip):
    nslab_total, nj, _, _ = u_tiles.shape
    nslab = SSM_STEPS // TIME_TILE
    rows = SSM_STEPS * batch
    nstate = a_re.shape[1]
    tile_spec = pl.BlockSpec((nslab, nj, batch * TIME_TILE, LANES), lambda t: (t, 0, 0, 0))
    full = lambda a: pl.BlockSpec(a.shape, lambda t: (0,) * a.ndim)
    d2 = d_skip
    return pl.pallas_call(
        functools.partial(_ssm_kernel, batch=batch),
        grid=(nslab_total // nslab,),
        in_specs=[tile_spec, full(w_b), full(w_cre), full(w_cim), full(a_re), full(a_im), full(d2)],
        out_specs=tile_spec,
        out_shape=jax.ShapeDtypeStruct(u_tiles.shape, F32),
        scratch_shapes=[
            pltpu.VMEM((rows, nj * LANES), F32),
            pltpu.VMEM((batch, nstate), F32), pltpu.VMEM((batch, nstate), F32),
        ],
        compiler_params=pltpu.CompilerParams(
            dimension_semantics=("arbitrary",), vmem_limit_bytes=VMEM_LIMIT),
        name="ssm",
    )(u_tiles, w_b, w_cre, w_cim, a_re, a_im, d2)


def _out_kernel(attn_ref, y_ref, gs_ref, x_ref, p_ref, wglu32_ref, bglu_ref, wout32_ref, gple_ref,
                wg32_ref, wp32_ref, o_ref, wglu_ref, wout_ref, wg_ref, wp_ref):
    _cast_once([(wglu32_ref, wglu_ref), (wout32_ref, wout_ref), (wg32_ref, wg_ref), (wp32_ref, wp_ref)])
    nslab, nj = y_ref.shape[0], y_ref.shape[1]
    rows = OUT_CHUNK_ROWS
    aw = attn_ref.shape[-1]

    def chunk(c):
        r = pl.ds(c * rows, rows)
        slabs = slice(c * rows // TIME_TILE, (c + 1) * rows // TIME_TILE)
        yg = jnp.concatenate([y_ref[slabs, j].reshape(rows, LANES) for j in range(nj)], axis=1)
        glu = jnp.dot(yg.astype(BF16), wglu_ref[...], preferred_element_type=F32) + bglu_ref[...]
        pp = jnp.dot(p_ref[0, r, :].astype(BF16), wp_ref[...], preferred_element_type=F32)
        yield
        gs = gs_ref[0, r, :]
        ssm = yg * _sigmoid(glu) * (gs * _sigmoid(gs))
        h = (x_ref[0, r, :]
             + jnp.dot(attn_ref[0, r, :], wout_ref[:aw, :], preferred_element_type=F32)
             + jnp.dot(ssm.astype(BF16), wout_ref[aw:, :], preferred_element_type=F32))
        yield
        ms = jnp.mean(h * h, axis=-1, keepdims=True)
        hn = (h * lax.rsqrt(ms + EPS) * gple_ref[...]).astype(BF16)
        yield
        gate = _sigmoid(jnp.dot(hn, wg_ref[...], preferred_element_type=F32))
        yield
        o_ref[0, r, :] = h + gate * pp

    _interleave([chunk(c) for c in range(nslab * TIME_TILE // rows)])


def _out_stage(attn, y_tiles, gate_s, x, p, w_glu, b_glu, w_out, ple_gain, w_gate, w_proj):
    b, s, d = x.shape
    rows = OUT_ROW_TILE
    nj = y_tiles.shape[1]
    nat = lambda a: pl.BlockSpec((1, rows, a.shape[-1]), lambda i, t: (i, t, 0))
    full = lambda a: pl.BlockSpec(a.shape, lambda i, t: (0,) * a.ndim, pipeline_mode=pl.Buffered(1))
    y_spec = pl.BlockSpec((rows // TIME_TILE, nj, TIME_TILE, LANES), lambda i, t: (t, 0, i, 0))
    b2, g2 = b_glu, ple_gain
    return pl.pallas_call(
        _out_kernel,
        grid=(b, s // rows),
        in_specs=[nat(attn), y_spec, nat(gate_s), nat(x), nat(p), full(w_glu), full(b2), full(w_out),
                  full(g2), full(w_gate), full(w_proj)],
        out_specs=nat(x),
        out_shape=jax.ShapeDtypeStruct(x.shape, x.dtype),
        scratch_shapes=[pltpu.VMEM(w.shape, BF16) for w in (w_glu, w_out, w_gate, w_proj)],
        compiler_params=pltpu.CompilerParams(
            dimension_semantics=("arbitrary", "arbitrary"), vmem_limit_bytes=VMEM_LIMIT),
        name="outstage",
    )(attn, y_tiles, gate_s, x, p, w_glu, b2, w_out, g2, w_gate, w_proj)


def kernel(x, p, mix_norm, w_in, q_norm, k_norm, lambda_re, lambda_im, log_dt, b_re, b_im, c_re, c_im,
           d_skip, w_glu, b_glu, w_out, ple_norm, w_ple_gate, w_ple_proj):
    batch = x.shape[0]
    h = x
    for i in range(w_in.shape[0]):
        row = slice(i, i + 1)
        q, k, v, gate_a, u_tiles, gate_s = _inproj(h, mix_norm[row], w_in[i])
        a_re, a_im, w_b, w_cre, w_cim = _ssm_params(
            lambda_re[i], lambda_im[i], log_dt[i], b_re[i], b_im[i], c_re[i], c_im[i])
        attn = _attention(q, k, v, gate_a, q_norm[i], k_norm[i])
        y_tiles = _ssm(u_tiles, batch, w_b, w_cre, w_cim, a_re, a_im, d_skip[row])
        h = _out_stage(attn, y_tiles, gate_s, h, p[i], w_glu[i], b_glu[row], w_out[i], ple_norm[row],
                       w_ple_gate[i], w_ple_proj[i])
    return h
```

```python
import functools
import math

import jax
import jax.numpy as jnp
from jax import lax
from jax.experimental import pallas as pl
from jax.experimental.pallas import tpu as pltpu

F32 = jnp.float32
BF16 = jnp.bfloat16

HEAD_DIM = 64
SSM_GROUP = 16
SSM_STATE = 64
DILATED_CONFIGS = ((128, 1), (512, 4), (2048, 16))
EPS = 1e-6

LANES = 128
QBLK = 128
NEG = -1e30
LOG2E = math.log2(math.e)
IN_ROW_TILE = 1024
IN_CHUNK_ROWS = 256
OUT_ROW_TILE = 1024
OUT_CHUNK_ROWS = 256
SSM_STEPS = 128
SSM_SUB = 32
TIME_TILE = 8
GROUPS_PER_CHUNK = LANES // SSM_GROUP
STATE_CHUNK = GROUPS_PER_CHUNK * SSM_STATE
VMEM_LIMIT = 56 * 1024 * 1024


def _sigmoid(x):
    return jax.nn.sigmoid(x)


def _advance(gens):
    for g in list(gens):
        try:
            next(g)
        except StopIteration:
            gens.remove(g)


def _cast_once(pairs):
    first = functools.reduce(jnp.logical_and, [pl.program_id(a) == 0 for a in range(2)])

    @pl.when(first)
    def _():
        for src, dst in pairs:
            for c in range(0, src.shape[1], 4 * LANES):
                dst[:, c:c + 4 * LANES] = src[:, c:c + 4 * LANES].astype(dst.dtype)


def _interleave(tasks):
    pending, active = list(tasks), []
    while pending or active:
        if pending:
            active.append(pending.pop(0))
        _advance(active)


def _inproj_kernel(x_ref, g_ref, w32_ref, q_ref, k_ref, v_ref, ga_ref, u_ref, gs_ref, w_ref, *, width):
    _cast_once([(w32_ref, w_ref)])
    rows = IN_CHUNK_ROWS

    def chunk(c):
        r = pl.ds(c * rows, rows)
        x = x_ref[0, r, :]
        ms = jnp.mean(x * x, axis=-1, keepdims=True)
        xn = (x * lax.rsqrt(ms + EPS) * g_ref[...]).astype(BF16)
        yield

        def proj(idx):
            return jnp.dot(xn, w_ref[:, idx * width:(idx + 1) * width], preferred_element_type=F32)

        q_ref[0, r, :] = proj(0)
        k_ref[0, r, :] = proj(1)
        yield
        v_ref[0, r, :] = proj(2)
        ga_ref[0, r, :] = proj(3)
        yield
        u = proj(4)
        for t in range(rows // TIME_TILE):
            for j in range(width // LANES):
                u_ref[c * rows // TIME_TILE + t, j] = u[t * TIME_TILE:(t + 1) * TIME_TILE, j * LANES:(j + 1) * LANES]
        gs_ref[0, r, :] = proj(5)

    _interleave([chunk(c) for c in range(x_ref.shape[1] // rows)])


def _inproj(x, gain, w):
    b, s, d = x.shape
    width = w.shape[1] // 6
    nj = width // LANES
    rows = IN_ROW_TILE
    nat = jax.ShapeDtypeStruct((b, s, width), F32)
    nat_spec = pl.BlockSpec((1, rows, width), lambda i, t: (i, t, 0))
    u_shape = jax.ShapeDtypeStruct((s // TIME_TILE, nj, b * TIME_TILE, LANES), F32)
    u_spec = pl.BlockSpec((rows // TIME_TILE, nj, TIME_TILE, LANES), lambda i, t: (t, 0, i, 0))
    return pl.pallas_call(
        functools.partial(_inproj_kernel, width=width),
        grid=(b, s // rows),
        in_specs=[
            pl.BlockSpec((1, rows, d), lambda i, t: (i, t, 0)),
            pl.BlockSpec((1, d), lambda i, t: (0, 0)),
            pl.BlockSpec(w.shape, lambda i, t: (0, 0), pipeline_mode=pl.Buffered(1)),
        ],
        out_specs=[nat_spec, nat_spec, nat_spec, nat_spec, u_spec, nat_spec],
        out_shape=[nat, nat, nat, nat, u_shape, nat],
        scratch_shapes=[pltpu.VMEM(w.shape, BF16)],
        compiler_params=pltpu.CompilerParams(
            dimension_semantics=("arbitrary", "arbitrary"), vmem_limit_bytes=VMEM_LIMIT),
        name="inproj",
    )(x, gain, w)


def _zoh(lam_re, lam_im, dt):
    mag = jnp.exp(lam_re * dt)
    a_re = mag * jnp.cos(lam_im * dt)
    a_im = mag * jnp.sin(lam_im * dt)
    den = lam_re * lam_re + lam_im * lam_im
    num_re = a_re - 1.0
    coef_re = (num_re * lam_re + a_im * lam_im) / den
    coef_im = (a_im * lam_re - num_re * lam_im) / den
    return a_re, a_im, coef_re, coef_im


def _zoh_kernel(lre_ref, lim_ref, ldt_ref, bre_ref, bim_ref, are_ref, aim_ref, bbre_ref, bbim_ref):
    g, n = lre_ref.shape
    c = bre_ref.shape[0] // g
    dt = jnp.exp(ldt_ref[...])
    a_re, a_im, coef_re, coef_im = _zoh(lre_ref[...], lim_ref[...], dt)
    are_ref[...] = a_re
    aim_ref[...] = a_im
    rows = lambda a: jnp.broadcast_to(a[:, None, :], (g, c, n)).reshape(g * c, n)
    coef_re, coef_im = rows(coef_re), rows(coef_im)
    b_re = bre_ref[...]
    b_im = bim_ref[...]
    bbre_ref[...] = coef_re * b_re - coef_im * b_im
    bbim_ref[...] = coef_re * b_im + coef_im * b_re


def _ssm_params(lam_re, lam_im, log_dt, b_re, b_im, c_re, c_im):
    g, n = lam_re.shape
    c = b_re.shape[-1]
    rows_gc = lambda b: b.transpose(0, 2, 1).reshape(g * c, n)
    small = jax.ShapeDtypeStruct((g, n), F32)
    tall = jax.ShapeDtypeStruct((g * c, n), F32)
    a_re, a_im, bb_re, bb_im = pl.pallas_call(
        _zoh_kernel, out_shape=[small, small, tall, tall], name="zoh",
    )(lam_re, lam_im, log_dt.reshape(g, 1), rows_gc(b_re), rows_gc(b_im))

    nch = g // GROUPS_PER_CHUNK

    def blockdiag(m, rows_per_group, cols_per_group):
        t = jnp.tile(m.reshape(nch, GROUPS_PER_CHUNK * rows_per_group, cols_per_group), (1, 1, GROUPS_PER_CHUNK))
        rg = lax.broadcasted_iota(jnp.int32, t.shape, 1) // rows_per_group
        cg = lax.broadcasted_iota(jnp.int32, t.shape, 2) // cols_per_group
        return jnp.where(rg == cg, t, 0.0)

    w_b = jnp.concatenate([blockdiag(bb_re, c, n), blockdiag(bb_im, c, n)], axis=-1).astype(BF16)
    rows_gn = lambda cc: cc.transpose(0, 2, 1).reshape(g * n, c)
    return (a_re.reshape(1, g * n), a_im.reshape(1, g * n), w_b,
            blockdiag(rows_gn(c_re), n, c).astype(BF16), blockdiag(rows_gn(c_im), n, c).astype(BF16))


def _attn_kernel(q_ref, k_ref, v_ref, ga_ref, gq_ref, gk_ref, o_ref,
                 qn, kn, ops_a, ops_b, res, out32, bias_full, bias_first, bias_mod4):
    seq = qn.shape[0]
    nblk = seq // QBLK
    per_class = nblk // 4
    assert DILATED_CONFIGS == ((128, 1), (512, 4), (2048, 16)) and nblk == 16
    lane = lax.broadcasted_iota(jnp.int32, (1, LANES), 1)
    head0 = lane < HEAD_DIM
    rr = lax.broadcasted_iota(jnp.int32, (2 * LANES, 2 * LANES), 0) // HEAD_DIM
    cc = lax.broadcasted_iota(jnp.int32, (2 * LANES, 2 * LANES), 1) // HEAD_DIM
    head_mean = jnp.where(rr == cc, 1.0 / HEAD_DIM, 0.0).astype(BF16)
    gq = gq_ref[...] * (HEAD_DIM ** -0.5 * LOG2E)
    gk = gk_ref[...]
    one0 = jnp.where(head0, 1.0, 0.0).astype(BF16)
    one1 = jnp.where(head0, 0.0, 1.0).astype(BF16)

    def blk(i, n=QBLK):
        return pl.ds(i * QBLK, n)

    def write_operands(ops, b, qv, kv, vv):
        rows = blk(b)
        vb = vv.astype(BF16)
        kb = kv.astype(BF16)
        zero = jnp.zeros_like(vb)
        ops[0, rows, :] = qv.astype(BF16)
        ops[1, rows, :] = jnp.where(head0, vb, zero)
        ops[2, rows, :] = jnp.where(head0, zero, vb)
        ops[3, rows, :] = jnp.where(head0, kb, zero)
        ops[4, rows, :] = jnp.where(head0, zero, kb)

    def norm_block(b):
        rows = blk(b)
        q = q_ref[0, rows, :]
        k = k_ref[0, rows, :]
        ms = jnp.dot(jnp.concatenate([(q * q).astype(BF16), (k * k).astype(BF16)], axis=1), head_mean,
                     preferred_element_type=F32)
        qv = q * lax.rsqrt(ms[:, :LANES] + EPS) * gq
        kv = k * lax.rsqrt(ms[:, LANES:] + EPS) * gk
        qn[rows, :] = qv
        kn[rows, :] = kv
        write_operands(ops_a, b, qv, kv, v_ref[0, rows, :])

    def to_mod4(c):
        src = pl.ds((c // per_class) + (c % per_class) * (4 * QBLK), QBLK, stride=4)
        write_operands(ops_b, c, qn[src, :], kn[src, :], v_ref.at[0][src, :])

    @pl.when((pl.program_id(0) == 0) & (pl.program_id(1) == 0))
    def _():
        qi = lax.broadcasted_iota(jnp.int32, (QBLK, 2 * QBLK), 0)
        kj = lax.broadcasted_iota(jnp.int32, (QBLK, 2 * QBLK), 1)
        bias_full[...] = jnp.where((kj >= qi) & (kj <= qi + QBLK), 0.0, NEG)
        qi1 = lax.broadcasted_iota(jnp.int32, (QBLK, QBLK), 0)
        kj1 = lax.broadcasted_iota(jnp.int32, (QBLK, QBLK), 1)
        bias_first[...] = jnp.where(kj1 <= qi1, 0.0, NEG)
        same4 = ((qi1 - kj1) & 3) == 0
        far = jnp.where(same4, 0.0, NEG)
        prev = jnp.where(same4, jnp.where(kj1 >= qi1, 1.0, 0.0), jnp.where(kj1 >= qi1, 0.0, NEG))
        cur = jnp.where(kj1 <= qi1, jnp.where(same4, 1.0, 0.0), NEG)
        for t, tile in enumerate((far, far, prev, cur)):
            bias_mod4[:, t * QBLK:(t + 1) * QBLK] = tile

    def unit(ops, b, k0, nk, bias, out):
        krows = pl.ds(k0, nk)
        kcat = jnp.concatenate([ops[3, krows, :], ops[4, krows, :]], axis=0)
        s2 = lax.dot_general(ops[0, blk(b), :], kcat, (((1,), (1,)), ((), ())),
                             preferred_element_type=F32)
        yield
        ss, ms = [], []
        for h in range(2):
            s = s2[:, h * nk:(h + 1) * nk] + bias
            ms.append(jnp.max(s, axis=-1, keepdims=True))
            ss.append(s)
        yield
        es = [jnp.exp2(s - m).astype(BF16) for s, m in zip(ss, ms)]
        yield
        rhs = jnp.concatenate([
            jnp.concatenate([ops[1, krows, :], jnp.broadcast_to(one0, (nk, LANES))], axis=1),
            jnp.concatenate([ops[2, krows, :], jnp.broadcast_to(one1, (nk, LANES))], axis=1)], axis=0)
        o = jnp.dot(jnp.concatenate(es, axis=1), rhs, preferred_element_type=F32)
        yield
        out.extend([jnp.where(head0, ms[0], ms[1]), o[:, :LANES], o[:, LANES:]])

    def unit_natural(b):
        part = []
        if b == 0:
            yield from unit(ops_a, b, 0, QBLK, bias_first[...], part)
        else:
            yield from unit(ops_a, b, (b - 1) * QBLK, 2 * QBLK, bias_full[...], part)
        for k in range(3):
            res[k, blk(b), :] = part[k]
        yield
        to_mod4(b)

    def unit_mod4(b):
        r, i = b // per_class, b % per_class
        rows = pl.ds(r + i * (4 * QBLK), QBLK, stride=4)
        seg = r * per_class * QBLK
        near = min(i + 1, 2)
        part_near, part_far = [], []
        gens = [unit(ops_b, b, seg + (i + 1 - near) * QBLK, near * QBLK,
                     bias_mod4[:, (per_class - near) * QBLK:], part_near)]
        if i >= 2:
            gens.append(unit(ops_b, b, seg, (i - 1) * QBLK, bias_mod4[:, :(i - 1) * QBLK], part_far))
        while gens:
            _advance(gens)
            yield
        parts = [[res.at[k][rows, :] for k in range(3)], part_near] + ([part_far] if part_far else [])
        mx = functools.reduce(jnp.maximum, [p[0] for p in parts])
        ws = [jnp.exp2(p[0] - mx) for p in parts]
        num = functools.reduce(jnp.add, [w * p[1] for w, p in zip(ws, parts)])
        den = functools.reduce(jnp.add, [w * p[2] for w, p in zip(ws, parts)])
        g = ga_ref.at[0][rows, :]
        out32[rows, :] = num / den * (g * _sigmoid(g))

    def cast_out(i):
        o_ref[0, blk(i), :] = out32[blk(i), :].astype(o_ref.dtype)

    for b in range(nblk):
        norm_block(b)
    _interleave([unit_natural(b) for b in range(nblk)])
    _interleave([unit_mod4(b) for b in range(nblk)])
    for i in range(nblk):
        cast_out(i)


def _attention(q, k, v, gate, q_gain, k_gain):
    b, s, width = q.shape
    pairs = width // LANES
    spec = pl.BlockSpec((1, s, LANES), lambda i, h: (i, 0, h))
    gspec = pl.BlockSpec((1, LANES), lambda i, h: (0, 0))
    tile2 = lambda g: jnp.concatenate([g, g]).reshape(1, LANES)
    return pl.pallas_call(
        _attn_kernel,
        grid=(b, pairs),
        in_specs=[spec, spec, spec, spec, gspec, gspec],
        out_specs=spec,
        out_shape=jax.ShapeDtypeStruct((b, s, width), BF16),
        scratch_shapes=[
            pltpu.VMEM((s, LANES), F32), pltpu.VMEM((s, LANES), F32),
            pltpu.VMEM((5, s, LANES), BF16),
            pltpu.VMEM((5, s, LANES), BF16),
            pltpu.VMEM((3, s, LANES), F32),
            pltpu.VMEM((s, LANES), F32),
            pltpu.VMEM((QBLK, 2 * QBLK), F32), pltpu.VMEM((QBLK, QBLK), F32), pltpu.VMEM((QBLK, 4 * QBLK), F32),
        ],
        compiler_params=pltpu.CompilerParams(
            dimension_semantics=("arbitrary", "arbitrary"), vmem_limit_bytes=VMEM_LIMIT),
        name="attn",
    )(q, k, v, gate, tile2(q_gain), tile2(k_gain))


def _ssm_kernel(u_ref, wb_ref, wcr_ref, wci_ref, are_ref, aim_ref, d_ref, y_ref,
                ut, sre, sim, *, batch):
    nslab, nj = u_ref.shape[0], u_ref.shape[1]
    steps = SSM_SUB
    nsub = nslab * TIME_TILE // steps

    @pl.when(pl.program_id(0) == 0)
    def _():
        sre[...] = jnp.zeros_like(sre)
        sim[...] = jnp.zeros_like(sim)

    for c in range(nslab):
        for j in range(nj):
            src = u_ref.at[c, j]
            for t in range(TIME_TILE):
                ut[pl.ds((c * TIME_TILE + t) * batch, batch), j * LANES:(j + 1) * LANES] = (
                    src[pl.ds(t, batch, stride=TIME_TILE), :])

    sqrt_half = math.sqrt(0.5)

    def chunk(j, sub):
        r0s = sub * steps * batch
        sl = slice(j * STATE_CHUNK, (j + 1) * STATE_CHUNK)
        lanes = slice(j * LANES, (j + 1) * LANES)
        uj = ut[r0s:r0s + steps * batch, lanes]
        bu = jnp.dot(uj.astype(BF16), wb_ref[j], preferred_element_type=F32)
        yield
        a_re = jnp.broadcast_to(are_ref[:, sl], (batch, STATE_CHUNK))
        a_im = jnp.broadcast_to(aim_ref[:, sl], (batch, STATE_CHUNK))
        x_re, x_im = sre[:, sl], sim[:, sl]
        hist_re, hist_im = [], []
        for t in range(steps):
            b_re = bu[t * batch:(t + 1) * batch, :STATE_CHUNK]
            b_im = bu[t * batch:(t + 1) * batch, STATE_CHUNK:]
            x_re, x_im = a_re * x_re - a_im * x_im + b_re, a_re * x_im + a_im * x_re + b_im
            hist_re.append(x_re.astype(BF16))
            hist_im.append(x_im.astype(BF16))
        sre[:, sl] = x_re
        sim[:, sl] = x_im
        yield
        y = (jnp.dot(jnp.concatenate(hist_re, axis=0), wcr_ref[j], preferred_element_type=F32)
             - jnp.dot(jnp.concatenate(hist_im, axis=0), wci_ref[j], preferred_element_type=F32))
        yield
        y = y + d_ref[:, lanes] * uj
        yg = 0.5 * y * (1.0 + lax.erf(y * sqrt_half))
        for c in range(steps // TIME_TILE):
            dst = y_ref.at[sub * (steps // TIME_TILE) + c, j]
            for t in range(TIME_TILE):
                r0 = (c * TIME_TILE + t) * batch
                dst[pl.ds(t, batch, stride=TIME_TILE), :] = yg[r0:r0 + batch, :]

    _interleave([chunk(j, sub) for sub in range(nsub) for j in range(nj)])


def _ssm(u_tiles, batch, w_b, w_cre, w_cim, a_re, a_im, d_skip):
    nslab_total, nj, _, _ = u_tiles.shape
    nslab = SSM_STEPS // TIME_TILE
    rows = SSM_STEPS * batch
    nstate = a_re.shape[1]
    tile_spec = pl.BlockSpec((nslab, nj, batch * TIME_TILE, LANES), lambda t: (t, 0, 0, 0))
    full = lambda a: pl.BlockSpec(a.shape, lambda t: (0,) * a.ndim)
    d2 = d_skip
    return pl.pallas_call(
        functools.partial(_ssm_kernel, batch=batch),
        grid=(nslab_total // nslab,),
        in_specs=[tile_spec, full(w_b), full(w_cre), full(w_cim), full(a_re), full(a_im), full(d2)],
        out_specs=tile_spec,
        out_shape=jax.ShapeDtypeStruct(u_tiles.shape, F32),
        scratch_shapes=[
            pltpu.VMEM((rows, nj * LANES), F32),
            pltpu.VMEM((batch, nstate), F32), pltpu.VMEM((batch, nstate), F32),
        ],
        compiler_params=pltpu.CompilerParams(
            dimension_semantics=("arbitrary",), vmem_limit_bytes=VMEM_LIMIT),
        name="ssm",
    )(u_tiles, w_b, w_cre, w_cim, a_re, a_im, d2)


def _out_kernel(attn_ref, y_ref, gs_ref, x_ref, p_ref, wglu32_ref, bglu_ref, wout32_ref, gple_ref,
                wg32_ref, wp32_ref, o_ref, wglu_ref, wout_ref, wg_ref, wp_ref):
    _cast_once([(wglu32_ref, wglu_ref), (wout32_ref, wout_ref), (wg32_ref, wg_ref), (wp32_ref, wp_ref)])
    nslab, nj = y_ref.shape[0], y_ref.shape[1]
    rows = OUT_CHUNK_ROWS
    aw = attn_ref.shape[-1]

    def chunk(c):
        r = pl.ds(c * rows, rows)
        slabs = slice(c * rows // TIME_TILE, (c + 1) * rows // TIME_TILE)
        yg = jnp.concatenate([y_ref[slabs, j].reshape(rows, LANES) for j in range(nj)], axis=1)
        glu = jnp.dot(yg.astype(BF16), wglu_ref[...], preferred_element_type=F32) + bglu_ref[...]
        pp = jnp.dot(p_ref[0, r, :].astype(BF16), wp_ref[...], preferred_element_type=F32)
        yield
        gs = gs_ref[0, r, :]
        ssm = yg * _sigmoid(glu) * (gs * _sigmoid(gs))
        h = (x_ref[0, r, :]
             + jnp.dot(attn_ref[0, r, :], wout_ref[:aw, :], preferred_element_type=F32)
             + jnp.dot(ssm.astype(BF16), wout_ref[aw:, :], preferred_element_type=F32))
        yield
        ms = jnp.mean(h * h, axis=-1, keepdims=True)
        hn = (h * lax.rsqrt(ms + EPS) * gple_ref[...]).astype(BF16)
        yield
        gate = _sigmoid(jnp.dot(hn, wg_ref[...], preferred_element_type=F32))
        yield
        o_ref[0, r, :] = h + gate * pp

    _interleave([chunk(c) for c in range(nslab * TIME_TILE // rows)])


def _out_stage(attn, y_tiles, gate_s, x, p, w_glu, b_glu, w_out, ple_gain, w_gate, w_proj):
    b, s, d = x.shape
    rows = OUT_ROW_TILE
    nj = y_tiles.shape[1]
    nat = lambda a: pl.BlockSpec((1, rows, a.shape[-1]), lambda i, t: (i, t, 0))
    full = lambda a: pl.BlockSpec(a.shape, lambda i, t: (0,) * a.ndim, pipeline_mode=pl.Buffered(1))
    y_spec = pl.BlockSpec((rows // TIME_TILE, nj, TIME_TILE, LANES), lambda i, t: (t, 0, i, 0))
    b2, g2 = b_glu, ple_gain
    return pl.pallas_call(
        _out_kernel,
        grid=(b, s // rows),
        in_specs=[nat(attn), y_spec, nat(gate_s), nat(x), nat(p), full(w_glu), full(b2), full(w_out),
                  full(g2), full(w_gate), full(w_proj)],
        out_specs=nat(x),
        out_shape=jax.ShapeDtypeStruct(x.shape, x.dtype),
        scratch_shapes=[pltpu.VMEM(w.shape, BF16) for w in (w_glu, w_out, w_gate, w_proj)],
        compiler_params=pltpu.CompilerParams(
            dimension_semantics=("arbitrary", "arbitrary"), vmem_limit_bytes=VMEM_LIMIT),
        name="outstage",
    )(attn, y_tiles, gate_s, x, p, w_glu, b2, w_out, g2, w_gate, w_proj)


def kernel(x, p, mix_norm, w_in, q_norm, k_norm, lambda_re, lambda_im, log_dt, b_re, b_im, c_re, c_im,
           d_skip, w_glu, b_glu, w_out, ple_norm, w_ple_gate, w_ple_proj):
    batch = x.shape[0]
    h = x
    for i in range(w_in.shape[0]):
        row = slice(i, i + 1)
        q, k, v, gate_a, u_tiles, gate_s = _inproj(h, mix_norm[row], w_in[i])
        a_re, a_im, w_b, w_cre, w_cim = _ssm_params(
            lambda_re[i], lambda_im[i], log_dt[i], b_re[i], b_im[i], c_re[i], c_im[i])
        attn = _attention(q, k, v, gate_a, q_norm[i], k_norm[i])
        y_tiles = _ssm(u_tiles, batch, w_b, w_cre, w_cim, a_re, a_im, d_skip[row])
        h = _out_stage(attn, y_tiles, gate_s, h, p[i], w_glu[i], b_glu[row], w_out[i], ple_norm[row],
                       w_ple_gate[i], w_ple_proj[i])
    return h
```
